```python
import math
import jax, jax.numpy as jnp
from jax import lax
import numpy as np

D_MODEL = 1024
BATCH = 16
SEQ = 2048
DEPTH = 2

CHUNK = 64
Q_BLOCK = 128
D_MIX = D_MODEL
SB_HEADS = 8
SB_HEAD_DIM = 64
DSA_HEADS = 8
DSA_HEAD_DIM = 64
DSA_KV_RANK = 128
IDX_HEADS = 8
IDX_DIM = 64
DSA_MAX_TOPK = 256
D_FF = 4 * D_MODEL
N_BUCKETS = 32
MAX_DISTANCE = 128
EPS = 1e-6

SB_QKV = 3 * SB_HEADS * SB_HEAD_DIM
DSA_Q = DSA_HEADS * DSA_KV_RANK
DSA_KV = DSA_KV_RANK
IDX_Q = IDX_HEADS * IDX_DIM
IDX_K = IDX_DIM
IDX_W = IDX_HEADS
D_IN_PROJ = SB_QKV + DSA_Q + DSA_KV + IDX_Q + IDX_K + IDX_W
SPLITS = [SB_QKV, SB_QKV + DSA_Q, SB_QKV + DSA_Q + DSA_KV,
          SB_QKV + DSA_Q + DSA_KV + IDX_Q, SB_QKV + DSA_Q + DSA_KV + IDX_Q + IDX_K]

kernel_name = "hybrid_stickbreak_dsa_adaln_trunk"


def rmsnorm(x, g):
    xf = x.astype(jnp.float32)
    y = xf * lax.rsqrt(jnp.mean(xf * xf, axis=-1, keepdims=True) + EPS)
    return (y * g.astype(jnp.float32)).astype(x.dtype)


def t5_bucket(rel):
    nb = N_BUCKETS // 2
    max_exact = nb // 2
    base = jnp.where(rel > 0, nb, 0)
    n = jnp.abs(rel)
    nf = jnp.maximum(n, max_exact).astype(jnp.float32)
    large = max_exact + (jnp.log(nf / max_exact) / math.log(MAX_DISTANCE / max_exact)
                         * (nb - max_exact)).astype(jnp.int32)
    large = jnp.minimum(large, nb - 1)
    return base + jnp.where(n < max_exact, n, large)


def stick_breaking_attention(q, k, v):
    B, S, H, Dh = q.shape
    nb = S // Q_BLOCK
    qb = q.reshape(B, nb, Q_BLOCK, H, Dh).transpose(1, 0, 2, 3, 4)
    kpos = jnp.arange(S)
    scale = Dh ** -0.5

    def block(args):
        qi, i = args
        qpos = i * Q_BLOCK + jnp.arange(Q_BLOCK)
        z = jnp.einsum('bqhd,bshd->bhqs', qi, k).astype(jnp.float32) * scale
        causal = (kpos[None, :] < qpos[:, None])[None, None]
        log_beta = jax.nn.log_sigmoid(z)
        log_1m = jnp.where(causal, log_beta - z, 0.0)
        after = lax.cumsum(log_1m, axis=3, reverse=True) - log_1m
        w = jnp.where(causal, jnp.exp(log_beta + after), 0.0)
        return jnp.einsum('bhqs,bshd->bqhd', w.astype(v.dtype), v)

    out = lax.map(block, (qb, jnp.arange(nb)))
    return out.transpose(1, 0, 2, 3, 4).reshape(B, S, H, Dh)


def dsa_attention(q, kv, q_idx, k_idx, w_idx, w_uv, rel_bias):
    B, S, H, R = q.shape
    topk = min(DSA_MAX_TOPK, S // 4)
    nb = S // Q_BLOCK
    kchunk = jnp.arange(S) // CHUNK
    qb = q.reshape(B, nb, Q_BLOCK, H, R).transpose(1, 0, 2, 3, 4)
    qib = q_idx.reshape(B, nb, Q_BLOCK, IDX_HEADS, IDX_DIM).transpose(1, 0, 2, 3, 4)
    wib = w_idx.reshape(B, nb, Q_BLOCK, IDX_HEADS).transpose(1, 0, 2, 3)
    bias_tab = rel_bias.astype(jnp.float32)

    def block(args):
        qi, qii, wi, i = args
        qpos = i * Q_BLOCK + jnp.arange(Q_BLOCK)
        qchunk = qpos // CHUNK
        admissible = kchunk[None, :] <= qchunk[:, None]
        sc = jnp.einsum('bqjd,bsd->bqjs', qii, k_idx).astype(jnp.float32) * (IDX_DIM ** -0.5)
        score = jnp.einsum('bqjs,bqj->bqs', jax.nn.relu(sc),
                           wi.astype(jnp.float32) * (IDX_HEADS ** -0.5))
        score = jnp.where(admissible[None], score, -jnp.inf)
        _, idx = lax.top_k(score, topk)
        kv_sel = jax.vmap(lambda a, ix: a[ix])(kv, idx)
        valid = (idx // CHUNK) <= qchunk[None, :, None]
        bias = bias_tab[t5_bucket(idx - qpos[None, :, None])]
        logits = (jnp.einsum('bqhr,bqkr->bqhk', qi, kv_sel).astype(jnp.float32) * (R ** -0.5)
                  + bias.transpose(0, 1, 3, 2))
        logits = jnp.where(valid[:, :, None, :], logits, -jnp.inf)
        p = jax.nn.softmax(logits, axis=-1)
        o = jnp.einsum('bqhk,bqkr->bqhr', p.astype(kv.dtype), kv_sel)
        return jnp.einsum('bqhr,hrd->bqhd', o, w_uv)

    out = lax.map(block, (qb, qib, wib, jnp.arange(nb)))
    return out.transpose(1, 0, 2, 3, 4).reshape(B, S, H, DSA_HEAD_DIM)


def setup_inputs(seed: int = 0) -> dict:
    key = jax.random.key(seed)
    ks = jax.random.split(key, 20)
    f32 = jnp.float32

    def nrm(k, shape, scale):
        return jax.random.normal(k, shape, f32) * scale

    def gain(k, shape):
        return 1.0 + 0.02 * jax.random.normal(k, shape, f32)

    return {
        "x": nrm(ks[0], (BATCH, SEQ, D_MODEL), 1.0),
        "c": nrm(ks[1], (BATCH, D_MODEL), 1.0),
        "w_mod": nrm(ks[2], (DEPTH, D_MODEL, 6 * D_MODEL), 0.5 * D_MODEL ** -0.5),
        "b_mod": nrm(ks[3], (DEPTH, 6 * D_MODEL), 0.02),
        "g_attn": gain(ks[4], (DEPTH, D_MODEL)),
        "w_in": nrm(ks[5], (DEPTH, D_MODEL, D_IN_PROJ), D_MODEL ** -0.5),
        "kv_norm_g": gain(ks[6], (DEPTH, DSA_KV_RANK)),
        "w_uv": nrm(ks[7], (DEPTH, DSA_HEADS, DSA_KV_RANK, DSA_HEAD_DIM), DSA_KV_RANK ** -0.5),
        "g_out_a": gain(ks[8], (DEPTH, SB_HEADS * SB_HEAD_DIM)),
        "g_out_b": gain(ks[9], (DEPTH, DSA_HEADS * DSA_HEAD_DIM)),
        "w_out": nrm(ks[10], (DEPTH, D_MIX, D_MODEL), D_MIX ** -0.5),
        "g_mlp": gain(ks[11], (DEPTH, D_MODEL)),
        "w_up": nrm(ks[12], (DEPTH, D_MODEL, D_FF), D_MODEL ** -0.5),
        "w_down": nrm(ks[13], (DEPTH, D_FF, D_MODEL), D_FF ** -0.5),
        "rel_bias": nrm(ks[14], (N_BUCKETS, DSA_HEADS), 0.2),
        "g_final": gain(ks[15], (D_MODEL,)),
    }


def reference(x, c, w_mod, b_mod, g_attn, w_in, kv_norm_g, w_uv, g_out_a, g_out_b,
              w_out, g_mlp, w_up, w_down, rel_bias, g_final):
    B, S, D = x.shape
    c_act = jax.nn.silu(c)
    for l in range(DEPTH):
        mod = c_act @ w_mod[l] + b_mod[l]
        sh1, sc1, ga1, sh2, sc2, ga2 = [m[:, None, :] for m in jnp.split(mod, 6, axis=-1)]

        h = rmsnorm(x, g_attn[l]) * (1.0 + sc1) + sh1
        proj = h @ w_in[l]
        sb_qkv, dsa_q, dsa_kv, idx_q, idx_k, idx_w = jnp.split(proj, SPLITS, axis=-1)
        q_a, k_a, v_a = [t.reshape(B, S, SB_HEADS, SB_HEAD_DIM)
                         for t in jnp.split(sb_qkv, 3, axis=-1)]
        o_a = stick_breaking_attention(q_a, k_a, v_a).reshape(B, S, SB_HEADS * SB_HEAD_DIM)

        kv_lat = rmsnorm(dsa_kv, kv_norm_g[l])
        o_b = dsa_attention(dsa_q.reshape(B, S, DSA_HEADS, DSA_KV_RANK), kv_lat,
                            idx_q.reshape(B, S, IDX_HEADS, IDX_DIM), idx_k, idx_w,
                            w_uv[l], rel_bias).reshape(B, S, DSA_HEADS * DSA_HEAD_DIM)

        o = jnp.concatenate([rmsnorm(o_a, g_out_a[l]), rmsnorm(o_b, g_out_b[l])], axis=-1)
        x = x + ga1 * (o @ w_out[l])

        h = rmsnorm(x, g_mlp[l]) * (1.0 + sc2) + sh2
        x = x + ga2 * (jnp.square(jax.nn.relu(h @ w_up[l])) @ w_down[l])
    return rmsnorm(x, g_final)
```

```python
import functools
import math

import numpy as np
import jax
import jax.numpy as jnp
from jax import lax
from jax.experimental import pallas as pl
from jax.experimental.pallas import tpu as pltpu

F32 = jnp.float32
BF16 = jnp.bfloat16
I32 = jnp.int32

D_MODEL = 1024
DEPTH = 2
CHUNK = 64
SB_HEADS = 8
SB_HEAD_DIM = 64
DSA_HEADS = 8
DSA_HEAD_DIM = 64
DSA_KV_RANK = 128
IDX_HEADS = 8
IDX_DIM = 64
DSA_MAX_TOPK = 256
D_FF = 4 * D_MODEL
N_BUCKETS = 32
MAX_DISTANCE = 128
EPS = 1e-6

D_SB = SB_HEADS * SB_HEAD_DIM
D_DSA_Q = DSA_HEADS * DSA_KV_RANK
D_IDX_Q = IDX_HEADS * IDX_DIM
D_DSA_OUT = DSA_HEADS * DSA_HEAD_DIM

LANES = 128
SUBLANES = 8
ATT_TILE = 256
TOKEN_TILE = 512
FF_TILE = 1024
MOD_TILE = 1024
VMEM_LIMIT = 56 * 1024 * 1024

NEG_BIG = -1e30
INT_MIN = -2 ** 31

C_QA, C_KA, C_VA = 0, D_SB, 2 * D_SB
C_DQ = 3 * D_SB
C_KV = C_DQ + D_DSA_Q
C_IQ = C_KV + DSA_KV_RANK
C_IK = C_IQ + D_IDX_Q
C_IW = C_IK + LANES
D_CAT = C_IW + LANES


def _nt(a, b):
    return lax.dot_general(a, b, (((1,), (1,)), ((), ())), preferred_element_type=F32)


def _dot(a, b):
    return jnp.dot(a, b, preferred_element_type=F32)


def _rms(v, axis=-1):
    return v * lax.rsqrt(jnp.mean(v * v, axis=axis, keepdims=True) + EPS)


def _mod_body(c_ref, w_ref, b_ref, o_ref):
    c = c_ref[...]
    ca = c / (1.0 + jnp.exp(-c))
    c_hi = ca.astype(BF16)
    c_lo = (ca - c_hi.astype(F32)).astype(BF16)
    w = w_ref[0]
    w_hi = w.astype(BF16)
    w_lo = (w - w_hi.astype(F32)).astype(BF16)
    o_ref[0] = _dot(c_hi, w_hi) + _dot(c_lo, w_hi) + _dot(c_hi, w_lo) + b_ref[0]


def _mod_call(c, w_mod, b_mod):
    depth, d, n = w_mod.shape
    bsz = c.shape[0]
    return pl.pallas_call(
        _mod_body,
        grid=(depth, n // MOD_TILE),
        in_specs=[
            pl.BlockSpec((bsz, d), lambda l, j: (0, 0)),
            pl.BlockSpec((1, d, MOD_TILE), lambda l, j: (l, 0, j)),
            pl.BlockSpec((1, 1, MOD_TILE), lambda l, j: (l, 0, j)),
        ],
        out_specs=pl.BlockSpec((1, bsz, MOD_TILE), lambda l, j: (l, 0, j)),
        out_shape=jax.ShapeDtypeStruct((depth, bsz, n), F32),
        compiler_params=pltpu.CompilerParams(vmem_limit_bytes=VMEM_LIMIT),
        name="adaln_mod",
    )(c, w_mod, b_mod.reshape(depth, 1, n))


def _inproj_body(x_ref, mod_ref, g_ref, w_ref, kvg_ref,
                 qa_ref, ka_ref, va_ref, dq_ref, kv_ref, kvt_ref, iq_ref, ik_ref, iwt_ref):
    x = x_ref[0]
    sh1 = mod_ref[0, 0:1, :]
    sc1 = mod_ref[0, 1:2, :]
    h = (_rms(x) * g_ref[...]) * (1.0 + sc1) + sh1
    h = h.astype(BF16)

    def proj(c0, width):
        return _dot(h, w_ref[:, c0:c0 + width])

    qa_ref[0] = (proj(C_QA, D_SB) * (SB_HEAD_DIM ** -0.5)).astype(BF16)
    ka_ref[0] = proj(C_KA, D_SB).astype(BF16)
    va_ref[0] = proj(C_VA, D_SB).astype(BF16)
    for hd in range(DSA_HEADS):
        dq_ref[0, hd] = (proj(C_DQ + hd * DSA_KV_RANK, DSA_KV_RANK)
                         * (DSA_KV_RANK ** -0.5)).astype(BF16)
    kv = _rms(proj(C_KV, DSA_KV_RANK)) * kvg_ref[...]
    kv_ref[0] = kv.astype(BF16)
    kv_t = kv.T.astype(BF16)
    for cblk in range(kvt_ref.shape[1]):
        kvt_ref[0, cblk] = kv_t[:, cblk * ATT_TILE:(cblk + 1) * ATT_TILE]
    iq_ref[0] = proj(C_IQ, D_IDX_Q).astype(BF16)
    ik_ref[0] = proj(C_IK, LANES).astype(BF16)
    iwt_ref[0] = proj(C_IW, LANES).T[0:IDX_HEADS, :]


def _inproj_call(x, modl, g, wcat, kvg):
    bsz, s, d = x.shape
    tm = min(TOKEN_TILE, s)
    nkb = tm // ATT_TILE
    out_shapes = (
        jax.ShapeDtypeStruct((bsz, s, D_SB), BF16),
        jax.ShapeDtypeStruct((bsz, s, D_SB), BF16),
        jax.ShapeDtypeStruct((bsz, s, D_SB), BF16),
        jax.ShapeDtypeStruct((bsz, DSA_HEADS, s, DSA_KV_RANK), BF16),
        jax.ShapeDtypeStruct((bsz, s, DSA_KV_RANK), BF16),
        jax.ShapeDtypeStruct((bsz, s // ATT_TILE, DSA_KV_RANK, ATT_TILE), BF16),
        jax.ShapeDtypeStruct((bsz, s, D_IDX_Q), BF16),
        jax.ShapeDtypeStruct((bsz, s, LANES), BF16),
        jax.ShapeDtypeStruct((bsz, IDX_HEADS, s), F32),
    )
    out_specs = (
        pl.BlockSpec((1, tm, D_SB), lambda b, i: (b, i, 0)),
        pl.BlockSpec((1, tm, D_SB), lambda b, i: (b, i, 0)),
        pl.BlockSpec((1, tm, D_SB), lambda b, i: (b, i, 0)),
        pl.BlockSpec((1, DSA_HEADS, tm, DSA_KV_RANK), lambda b, i: (b, 0, i, 0)),
        pl.BlockSpec((1, tm, DSA_KV_RANK), lambda b, i: (b, i, 0)),
        pl.BlockSpec((1, nkb, DSA_KV_RANK, ATT_TILE), lambda b, i: (b, i, 0, 0)),
        pl.BlockSpec((1, tm, D_IDX_Q), lambda b, i: (b, i, 0)),
        pl.BlockSpec((1, tm, LANES), lambda b, i: (b, i, 0)),
        pl.BlockSpec((1, IDX_HEADS, tm), lambda b, i: (b, 0, i)),
    )
    return pl.pallas_call(
        _inproj_body,
        grid=(bsz, s // tm),
        in_specs=[
            pl.BlockSpec((1, tm, d), lambda b, i: (b, i, 0)),
            pl.BlockSpec((1, 6, d), lambda b, i: (b, 0, 0)),
            pl.BlockSpec((1, d), lambda b, i: (0, 0)),
            pl.BlockSpec((d, D_CAT), lambda b, i: (0, 0)),
            pl.BlockSpec((1, DSA_KV_RANK), lambda b, i: (0, 0)),
        ],
        out_specs=out_specs,
        out_shape=out_shapes,
        compiler_params=pltpu.CompilerParams(
            dimension_semantics=("parallel", "parallel"), vmem_limit_bytes=VMEM_LIMIT),
        name="inproj",
    )(x, modl, g.reshape(1, d), wcat, kvg.reshape(1, DSA_KV_RANK))


def _sb_body(q_ref, k_ref, v_ref, tri_ref, o_ref, *, t):
    i = pl.program_id(2)
    q2 = q_ref[0]
    lane = lax.broadcasted_iota(I32, (1, LANES), 1)
    tri = tri_ref[...]
    row = lax.broadcasted_iota(I32, (t, t), 0)
    col = lax.broadcasted_iota(I32, (t, t), 1)
    causal = col < row

    outs = []
    for half in range(2):
        in_half = (lane < SB_HEAD_DIM) if half == 0 else (lane >= SB_HEAD_DIM)

        def tile(j, carry, acc, diagonal, in_half=in_half):
            start = pl.multiple_of(j * t, t)
            kb = k_ref[0, pl.ds(start, t), :]
            vb = v_ref[0, pl.ds(start, t), :]
            kb = jnp.where(in_half, kb, jnp.zeros_like(kb))
            z = _nt(q2, kb)
            sp = jnp.log(1.0 + jnp.exp(-jnp.abs(z)))
            log_beta = jnp.minimum(z, 0.0) - sp
            log_1m = -(jnp.maximum(z, 0.0) + sp)
            if diagonal:
                log_1m = jnp.where(causal, log_1m, 0.0)
            after = _dot(log_1m.astype(BF16), tri)
            w = jnp.exp(log_beta + after + carry)
            if diagonal:
                w = jnp.where(causal, w, 0.0)
            acc = acc + _dot(w.astype(BF16), vb)
            carry = carry + jnp.sum(log_1m, axis=-1, keepdims=True)
            return carry, acc

        carry, acc = tile(i, jnp.zeros((t, 1), F32), jnp.zeros((t, LANES), F32), True)
        carry, acc = lax.fori_loop(
            0, i, lambda n, ca: tile(i - 1 - n, ca[0], ca[1], False), (carry, acc))
        outs.append(acc)
    o_ref[0] = jnp.where(lane < SB_HEAD_DIM, outs[0], outs[1])


def _sb_call(qa, ka, va, tri):
    bsz, s, _ = qa.shape
    t = ATT_TILE
    return pl.pallas_call(
        functools.partial(_sb_body, t=t),
        grid=(bsz, D_SB // LANES, s // t),
        in_specs=[
            pl.BlockSpec((1, t, LANES), lambda b, hp, i: (b, i, hp)),
            pl.BlockSpec((1, s, LANES), lambda b, hp, i: (b, 0, hp)),
            pl.BlockSpec((1, s, LANES), lambda b, hp, i: (b, 0, hp)),
            pl.BlockSpec((t, t), lambda b, hp, i: (0, 0)),
        ],
        out_specs=pl.BlockSpec((1, t, LANES), lambda b, hp, i: (b, i, hp)),
        out_shape=jax.ShapeDtypeStruct((bsz, s, D_SB), F32),
        compiler_params=pltpu.CompilerParams(
            dimension_semantics=("parallel", "parallel", "parallel"), vmem_limit_bytes=VMEM_LIMIT),
        name="stickbreak_attn",
    )(qa, ka, va, tri)


def _dsa_body(dq_ref, kv_ref, kvt_ref, iq_ref, ik_ref, iwt_ref, bias_ref, wuvt_ref, o_ref,
              key_ref, madd_ref, obt_ref, *, t, topk, idx_bits):
    i = pl.program_id(1)
    nblk = i + 1
    lane = lax.broadcasted_iota(I32, (1, LANES), 1)
    lo_half = lane < IDX_DIM
    srow = lax.broadcasted_iota(I32, (t, t), 0)
    qcol = lax.broadcasted_iota(I32, (t, t), 1)

    def score_keys(j):
        start = pl.multiple_of(j * t, t)
        ikb = ik_ref[0, pl.ds(start, t), :]
        zero = jnp.zeros_like(ikb)
        ik_half = (jnp.where(lo_half, ikb, zero), jnp.where(lo_half, zero, ikb))
        acc = jnp.zeros((t, t), F32)
        for hp in range(IDX_HEADS // 2):
            iq_pair = iq_ref[0, :, hp * LANES:(hp + 1) * LANES]
            for half in range(2):
                hd = 2 * hp + half
                sc = _nt(ik_half[half], iq_pair)
                acc = acc + jnp.maximum(sc, 0.0) * iwt_ref[0, hd:hd + 1, :]
        bits = lax.bitcast_convert_type(acc, I32)
        return jnp.where(bits < 0, jnp.int32(INT_MIN) - bits, bits), start

    def store_far(j, carry):
        keys, start = score_keys(j)
        key_ref[pl.ds(start, t), :] = keys
        return carry

    lax.fori_loop(0, i, store_far, 0)
    keys, start = score_keys(i)
    admissible = (srow // CHUNK) <= (qcol // CHUNK)
    key_ref[pl.ds(start, t), :] = jnp.where(admissible, keys, jnp.int32(INT_MIN))

    def col_count(pred_fn):
        def body(j, c8):
            kb = key_ref[pl.ds(pl.multiple_of(j * t, t), t), :]
            ind = pred_fn(kb, j).astype(I32)
            return c8 + jnp.sum(ind.reshape(t // SUBLANES, SUBLANES, t), axis=0)
        c8 = lax.fori_loop(0, nblk, body, jnp.zeros((SUBLANES, t), I32))
        return jnp.sum(c8, axis=0, keepdims=True)

    def value_bit(n, thr):
        cand = thr + lax.shift_left(jnp.int32(1), 31 - n)
        cnt = col_count(lambda kb, j: kb >= cand)
        return jnp.where(cnt >= topk, cand, thr)

    thr = lax.fori_loop(0, 32, value_bit, jnp.full((1, t), INT_MIN, I32))

    cnt_gt = col_count(lambda kb, j: kb > thr)
    cnt_eq = col_count(lambda kb, j: kb == thr)
    need = topk - cnt_gt
    real_thr = thr > INT_MIN
    surplus = jnp.logical_and(cnt_eq > need, real_thr)
    any_surplus = jnp.max(surplus.astype(I32)) > 0

    def index_cut():
        def index_bit(n, cut):
            cand = cut + lax.shift_left(jnp.int32(1), idx_bits - 1 - n)
            cnt = col_count(lambda kb, j: jnp.logical_and(kb == thr, srow + j * t < cand))
            return jnp.where(cnt <= need, cand, cut)
        return lax.fori_loop(0, idx_bits, index_bit, jnp.zeros((1, t), I32))

    cut = lax.cond(any_surplus, index_cut, lambda: jnp.full((1, t), 1 << idx_bits, I32))
    cut = jnp.where(real_thr, cut, 0)

    def store_mask(j, carry):
        start = pl.multiple_of(j * t, t)
        kb = key_ref[pl.ds(start, t), :]
        keep = jnp.logical_or(kb > thr, jnp.logical_and(kb == thr, srow + j * t < cut))
        madd_ref[pl.ds(start, t), :] = jnp.where(keep, 0.0, NEG_BIG)
        return carry

    lax.fori_loop(0, nblk, store_mask, 0)

    def head(hd, carry):
        qh = dq_ref[0, hd]

        def blk(j, state):
            m, l, acc = state
            start = pl.multiple_of(j * t, t)
            kvb = kv_ref[0, pl.ds(start, t), :]
            bidx = jnp.clip(j - i + 2, 0, 2)
            s = _nt(kvb, qh) + madd_ref[pl.ds(start, t), :] + bias_ref[bidx, hd]
            m_new = jnp.maximum(m, jnp.max(s, axis=0, keepdims=True))
            alpha = jnp.exp(m - m_new)
            p = jnp.exp(s - m_new)
            l = l * alpha + jnp.sum(p, axis=0, keepdims=True)
            acc = acc * alpha + _dot(kvt_ref[0, j], p.astype(BF16))
            return m_new, l, acc

        m, l, acc = lax.fori_loop(
            0, nblk, blk,
            (jnp.full((1, t), NEG_BIG, F32), jnp.zeros((1, t), F32), jnp.zeros((DSA_KV_RANK, t), F32)))
        o_t = (acc * (1.0 / l)).astype(BF16)
        out_t = _dot(wuvt_ref[hd], o_t)
        obt_ref[pl.ds(pl.multiple_of(hd * DSA_HEAD_DIM, DSA_HEAD_DIM), DSA_HEAD_DIM), :] = out_t
        return carry

    lax.fori_loop(0, DSA_HEADS, head, 0)
    o_ref[0] = obt_ref[...].T


def _dsa_call(dq, kv, kvt, iq, ik2, iwt, bias, wuvt):
    bsz, s, _ = kv.shape
    t = ATT_TILE
    nkb = s // t
    topk = min(DSA_MAX_TOPK, s // 4)
    body = functools.partial(_dsa_body, t=t, topk=topk, idx_bits=int(s).bit_length())
    return pl.pallas_call(
        body,
        grid=(bsz, nkb),
        in_specs=[
            pl.BlockSpec((1, DSA_HEADS, t, DSA_KV_RANK), lambda b, i: (b, 0, i, 0)),
            pl.BlockSpec((1, s, DSA_KV_RANK), lambda b, i: (b, 0, 0)),
            pl.BlockSpec((1, nkb, DSA_KV_RANK, t), lambda b, i: (b, 0, 0, 0)),
            pl.BlockSpec((1, t, D_IDX_Q), lambda b, i: (b, i, 0)),
            pl.BlockSpec((1, s, LANES), lambda b, i: (b, 0, 0)),
            pl.BlockSpec((1, IDX_HEADS, t), lambda b, i: (b, 0, i)),
            pl.BlockSpec((3, DSA_HEADS, t, t), lambda b, i: (0, 0, 0, 0)),
            pl.BlockSpec((DSA_HEADS, DSA_HEAD_DIM, DSA_KV_RANK), lambda b, i: (0, 0, 0)),
        ],
        out_specs=pl.BlockSpec((1, t, D_DSA_OUT), lambda b, i: (b, i, 0)),
        out_shape=jax.ShapeDtypeStruct((bsz, s, D_DSA_OUT), F32),
        scratch_shapes=[
            pltpu.VMEM((s, t), I32),
            pltpu.VMEM((s, t), F32),
            pltpu.VMEM((D_DSA_OUT, t), F32),
        ],
        compiler_params=pltpu.CompilerParams(
            dimension_semantics=("parallel", "parallel"), vmem_limit_bytes=VMEM_LIMIT),
        name="sparse_attn",
    )(dq, kv, kvt, iq, ik2, iwt, bias, wuvt)


def _post_body(oa_ref, ob_ref, x_ref, mod_ref, ga_ref, gb_ref, wout_ref, gm_ref, wup_ref, wdn_ref,
               gf_ref, o_ref, *, final):
    ga1 = mod_ref[0, 2:3, :]
    sh2 = mod_ref[0, 3:4, :]
    sc2 = mod_ref[0, 4:5, :]
    ga2 = mod_ref[0, 5:6, :]
    na = (_rms(oa_ref[0]) * ga_ref[...]).astype(BF16)
    nb = (_rms(ob_ref[0]) * gb_ref[...]).astype(BF16)
    attn = _dot(na, wout_ref[0:D_SB, :]) + _dot(nb, wout_ref[D_SB:, :])
    x1 = x_ref[0] + ga1 * attn
    h = ((_rms(x1) * gm_ref[...]) * (1.0 + sc2) + sh2).astype(BF16)
    acc = jnp.zeros_like(x1)
    for cblk in range(D_FF // FF_TILE):
        u = _dot(h, wup_ref[:, cblk * FF_TILE:(cblk + 1) * FF_TILE])
        r = jnp.maximum(u, 0.0)
        acc = acc + _dot((r * r).astype(BF16), wdn_ref[cblk * FF_TILE:(cblk + 1) * FF_TILE, :])
    x2 = x1 + ga2 * acc
    if final:
        x2 = _rms(x2) * gf_ref[...]
    o_ref[0] = x2


def _post_call(o_a, o_b, x, modl, g_a, g_b, w_out, g_mlp, w_up, w_down, g_final, final):
    bsz, s, d = x.shape
    tm = min(TOKEN_TILE, s)
    const2 = lambda b, i: (0, 0)
    return pl.pallas_call(
        functools.partial(_post_body, final=final),
        grid=(bsz, s // tm),
        in_specs=[
            pl.BlockSpec((1, tm, D_SB), lambda b, i: (b, i, 0)),
            pl.BlockSpec((1, tm, D_DSA_OUT), lambda b, i: (b, i, 0)),
            pl.BlockSpec((1, tm, d), lambda b, i: (b, i, 0)),
            pl.BlockSpec((1, 6, d), lambda b, i: (b, 0, 0)),
            pl.BlockSpec((1, D_SB), const2),
            pl.BlockSpec((1, D_DSA_OUT), const2),
            pl.BlockSpec((d, d), const2),
            pl.BlockSpec((1, d), const2),
            pl.BlockSpec((d, D_FF), const2),
            pl.BlockSpec((D_FF, d), const2),
            pl.BlockSpec((1, d), const2),
        ],
        out_specs=pl.BlockSpec((1, tm, d), lambda b, i: (b, i, 0)),
        out_shape=jax.ShapeDtypeStruct((bsz, s, d), F32),
        compiler_params=pltpu.CompilerParams(
            dimension_semantics=("parallel", "parallel"), vmem_limit_bytes=VMEM_LIMIT),
        name="outproj_mlp",
    )(o_a, o_b, x, modl, g_a.reshape(1, -1), g_b.reshape(1, -1), w_out, g_mlp.reshape(1, d),
      w_up, w_down, g_final.reshape(1, d))


def _t5_bucket_table(rel):
    nb = N_BUCKETS // 2
    max_exact = nb // 2
    n = np.abs(rel)
    nf = np.maximum(n, max_exact).astype(np.float64)
    large = max_exact + (np.log(nf / max_exact) / math.log(MAX_DISTANCE / max_exact)
                         * (nb - max_exact)).astype(np.int64)
    large = np.minimum(large, nb - 1)
    return np.where(rel > 0, nb, 0) + np.where(n < max_exact, n, large)


def _bias_tiles(rel_bias, t, s):
    far_bucket = _t5_bucket_table(np.arange(-(s - 1), -t + 1))
    assert t >= MAX_DISTANCE and np.all(far_bucket == far_bucket[0])
    ko = np.arange(t)[:, None]
    qo = np.arange(t)[None, :]
    b_diag = _t5_bucket_table(ko - qo)
    b_prev = _t5_bucket_table(ko - qo - t)
    tab = rel_bias.astype(F32)
    far = tab[int(far_bucket[0])]
    diag = jnp.transpose(tab[b_diag] - far, (2, 0, 1))
    prev = jnp.transpose(tab[b_prev] - far, (2, 0, 1))
    return jnp.stack([jnp.zeros_like(diag), prev, diag])


def _cat_weight(w_in_l):
    c_sb = 3 * D_SB
    c_dq = c_sb + D_DSA_Q
    c_kv = c_dq + DSA_KV_RANK
    c_iq = c_kv + D_IDX_Q
    c_ik = c_iq + IDX_DIM
    ik = w_in_l[:, c_iq:c_ik]
    iw = w_in_l[:, c_ik:]
    pad = jnp.zeros((w_in_l.shape[0], LANES - IDX_HEADS), w_in_l.dtype)
    return jnp.concatenate([w_in_l[:, :c_iq], ik, ik, iw, pad], axis=1).astype(BF16)


def kernel(x, c, w_mod, b_mod, g_attn, w_in, kv_norm_g, w_uv, g_out_a, g_out_b, w_out, g_mlp,
           w_up, w_down, rel_bias, g_final):
    bsz, s, d = x.shape
    assert d == D_MODEL and s % TOKEN_TILE == 0 and TOKEN_TILE % ATT_TILE == 0
    mod = _mod_call(c, w_mod, b_mod)
    bias = _bias_tiles(rel_bias, ATT_TILE, s)
    tri = jnp.asarray(np.tril(np.ones((ATT_TILE, ATT_TILE), np.float32), -1), BF16)
    for l in range(DEPTH):
        modl = mod[l].reshape(bsz, 6, d)
        qa, ka, va, dq, kv, kvt, iq, ik2, iwt = _inproj_call(
            x, modl, g_attn[l], _cat_weight(w_in[l]), kv_norm_g[l])
        o_a = _sb_call(qa, ka, va, tri)
        wuvt = jnp.transpose(w_uv[l], (0, 2, 1)).astype(BF16)
        o_b = _dsa_call(dq, kv, kvt, iq, ik2, iwt, bias, wuvt)
        x = _post_call(o_a, o_b, x, modl, g_out_a[l], g_out_b[l], w_out[l].astype(BF16), g_mlp[l],
                       w_up[l].astype(BF16), w_down[l].astype(BF16), g_final, l == DEPTH - 1)
    return x
```

```python
import functools
import math

import numpy as np
import jax
import jax.numpy as jnp
from jax import lax
from jax.experimental import pallas as pl
from jax.experimental.pallas import tpu as pltpu

F32 = jnp.float32
BF16 = jnp.bfloat16
I32 = jnp.int32

D_MODEL = 1024
DEPTH = 2
CHUNK = 64
SB_HEADS = 8
SB_HEAD_DIM = 64
DSA_HEADS = 8
DSA_HEAD_DIM = 64
DSA_KV_RANK = 128
IDX_HEADS = 8
IDX_DIM = 64
DSA_MAX_TOPK = 256
D_FF = 4 * D_MODEL
N_BUCKETS = 32
MAX_DISTANCE = 128
EPS = 1e-6

D_SB = SB_HEADS * SB_HEAD_DIM
D_DSA_Q = DSA_HEADS * DSA_KV_RANK
D_IDX_Q = IDX_HEADS * IDX_DIM
D_DSA_OUT = DSA_HEADS * DSA_HEAD_DIM

LANES = 128
SUBLANES = 8
ATT_TILE = 256
SB_Q_TILE = 2 * ATT_TILE
COUNT_CHAINS = 4
TOKEN_TILE = 512
FF_TILE = 1024
MOD_TILE = 1024
VMEM_LIMIT = 56 * 1024 * 1024

NEG_BIG = -1e30
INT_MIN = -2 ** 31
LOG2E = 1.4426950408889634

C_QA, C_KA, C_VA = 0, D_SB, 2 * D_SB
C_DQ = 3 * D_SB
C_KV = C_DQ + D_DSA_Q
C_IQ = C_KV + DSA_KV_RANK
C_IK = C_IQ + D_IDX_Q
C_IW = C_IK + LANES
D_CAT = C_IW + LANES


def _nt(a, b):
    return lax.dot_general(a, b, (((1,), (1,)), ((), ())), preferred_element_type=F32)


def _dot(a, b):
    return jnp.dot(a, b, preferred_element_type=F32)


def _rms(v, axis=-1):
    return v * lax.rsqrt(jnp.mean(v * v, axis=axis, keepdims=True) + EPS)


def _mod_body(c_ref, w_ref, b_ref, o_ref):
    c = c_ref[...]
    ca = c / (1.0 + jnp.exp(-c))
    c_hi = ca.astype(BF16)
    c_lo = (ca - c_hi.astype(F32)).astype(BF16)
    w = w_ref[0]
    w_hi = w.astype(BF16)
    w_lo = (w - w_hi.astype(F32)).astype(BF16)
    o_ref[0] = _dot(c_hi, w_hi) + _dot(c_lo, w_hi) + _dot(c_hi, w_lo) + b_ref[0]


def _mod_call(c, w_mod, b_mod):
    depth, d, n = w_mod.shape
    bsz = c.shape[0]
    return pl.pallas_call(
        _mod_body,
        grid=(depth, n // MOD_TILE),
        in_specs=[
            pl.BlockSpec((bsz, d), lambda l, j: (0, 0)),
            pl.BlockSpec((1, d, MOD_TILE), lambda l, j: (l, 0, j)),
            pl.BlockSpec((1, 1, MOD_TILE), lambda l, j: (l, 0, j)),
        ],
        out_specs=pl.BlockSpec((1, bsz, MOD_TILE), lambda l, j: (l, 0, j)),
        out_shape=jax.ShapeDtypeStruct((depth, bsz, n), F32),
        compiler_params=pltpu.CompilerParams(vmem_limit_bytes=VMEM_LIMIT),
        name="adaln_mod",
    )(c, w_mod, b_mod.reshape(depth, 1, n))


def _inproj_body(x_ref, mod_ref, g_ref, w_ref, kvg_ref,
                 qa_ref, ka_ref, va_ref, dq_ref, kv_ref, kvt_ref, iq_ref, ik_ref, iwt_ref):
    x = x_ref[0]
    sh1 = mod_ref[0, 0:1, :]
    sc1 = mod_ref[0, 1:2, :]
    h = (_rms(x) * g_ref[...]) * (1.0 + sc1) + sh1
    h = h.astype(BF16)

    def proj(c0, width):
        return _dot(h, w_ref[:, c0:c0 + width])

    qa_ref[0] = (proj(C_QA, D_SB) * (SB_HEAD_DIM ** -0.5 * LOG2E)).astype(BF16)
    ka_ref[0] = proj(C_KA, D_SB).astype(BF16)
    va_ref[0] = proj(C_VA, D_SB).astype(BF16)
    for hd in range(DSA_HEADS):
        dq_ref[0, hd] = (proj(C_DQ + hd * DSA_KV_RANK, DSA_KV_RANK)
                         * (DSA_KV_RANK ** -0.5 * LOG2E)).astype(BF16)
    kv = _rms(proj(C_KV, DSA_KV_RANK)) * kvg_ref[...]
    kv_ref[0] = kv.astype(BF16)
    kv_t = kv.T.astype(BF16)
    for cblk in range(kvt_ref.shape[1]):
        kvt_ref[0, cblk] = kv_t[:, cblk * ATT_TILE:(cblk + 1) * ATT_TILE]
    iq_ref[0] = proj(C_IQ, D_IDX_Q).astype(BF16)
    ik_ref[0] = proj(C_IK, LANES).astype(BF16)
    iwt_ref[0] = proj(C_IW, LANES).T[0:IDX_HEADS, :]


def _inproj_call(x, modl, g, wcat, kvg):
    bsz, s, d = x.shape
    tm = min(TOKEN_TILE, s)
    nkb = tm // ATT_TILE
    out_shapes = (
        jax.ShapeDtypeStruct((bsz, s, D_SB), BF16),
        jax.ShapeDtypeStruct((bsz, s, D_SB), BF16),
        jax.ShapeDtypeStruct((bsz, s, D_SB), BF16),
        jax.ShapeDtypeStruct((bsz, DSA_HEADS, s, DSA_KV_RANK), BF16),
        jax.ShapeDtypeStruct((bsz, s, DSA_KV_RANK), BF16),
        jax.ShapeDtypeStruct((bsz, s // ATT_TILE, DSA_KV_RANK, ATT_TILE), BF16),
        jax.ShapeDtypeStruct((bsz, s, D_IDX_Q), BF16),
        jax.ShapeDtypeStruct((bsz, s, LANES), BF16),
        jax.ShapeDtypeStruct((bsz, IDX_HEADS, s), F32),
    )
    out_specs = (
        pl.BlockSpec((1, tm, D_SB), lambda b, i: (b, i, 0)),
        pl.BlockSpec((1, tm, D_SB), lambda b, i: (b, i, 0)),
        pl.BlockSpec((1, tm, D_SB), lambda b, i: (b, i, 0)),
        pl.BlockSpec((1, DSA_HEADS, tm, DSA_KV_RANK), lambda b, i: (b, 0, i, 0)),
        pl.BlockSpec((1, tm, DSA_KV_RANK), lambda b, i: (b, i, 0)),
        pl.BlockSpec((1, nkb, DSA_KV_RANK, ATT_TILE), lambda b, i: (b, i, 0, 0)),
        pl.BlockSpec((1, tm, D_IDX_Q), lambda b, i: (b, i, 0)),
        pl.BlockSpec((1, tm, LANES), lambda b, i: (b, i, 0)),
        pl.BlockSpec((1, IDX_HEADS, tm), lambda b, i: (b, 0, i)),
    )
    return pl.pallas_call(
        _inproj_body,
        grid=(bsz, s // tm),
        in_specs=[
            pl.BlockSpec((1, tm, d), lambda b, i: (b, i, 0)),
            pl.BlockSpec((1, 6, d), lambda b, i: (b, 0, 0)),
            pl.BlockSpec((1, d), lambda b, i: (0, 0)),
            pl.BlockSpec((d, D_CAT), lambda b, i: (0, 0)),
            pl.BlockSpec((1, DSA_KV_RANK), lambda b, i: (0, 0)),
        ],
        out_specs=out_specs,
        out_shape=out_shapes,
        compiler_params=pltpu.CompilerParams(
            dimension_semantics=("parallel", "parallel"), vmem_limit_bytes=VMEM_LIMIT),
        name="inproj",
    )(x, modl, g.reshape(1, d), wcat, kvg.reshape(1, DSA_KV_RANK))


def _sb_body(q_ref, k_ref, v_ref, tri_ref, o_ref, acc_ref, *, tq, tk):
    i = pl.program_id(2)
    lane = lax.broadcasted_iota(I32, (1, LANES), 1)
    in_half = (lane < SB_HEAD_DIM, lane >= SB_HEAD_DIM)
    tri = tri_ref[...]
    acc_ref[...] = jnp.zeros(acc_ref.shape, F32)

    def tile(j, carries, r0, diag_off):
        start = pl.multiple_of(j * tk, tk)
        kb = k_ref[0, pl.ds(start, tk), :]
        vb = v_ref[0, pl.ds(start, tk), :]
        q2 = q_ref[0, r0:tq, :]
        if diag_off is not None:
            row = lax.broadcasted_iota(I32, (tq - r0, tk), 0) + r0
            col = lax.broadcasted_iota(I32, (tq - r0, tk), 1) + diag_off
            causal = col < row
        out = []
        for half in range(2):
            kh = jnp.where(in_half[half], kb, jnp.zeros_like(kb))
            z = _nt(q2, kh)
            neg_abs = lax.bitcast_convert_type(
                lax.bitcast_convert_type(z, I32) | jnp.int32(INT_MIN), F32)
            sp = jnp.log(1.0 + jnp.exp2(neg_abs)) * LOG2E
            log_beta = jnp.minimum(z, 0.0) - sp
            log_1m = log_beta - z
            if diag_off is not None:
                log_1m = jnp.where(causal, log_1m, 0.0)
            after = _dot(log_1m.astype(BF16), tri)
            carry = carries[half][r0:tq]
            w = jnp.exp2(log_beta + after + carry)
            if diag_off is not None:
                w = jnp.where(causal, w, 0.0)
            acc_ref[half, r0:tq, :] += _dot(w.astype(BF16), vb)
            carry = carry + jnp.sum(log_1m, axis=-1, keepdims=True)
            if r0 > 0:
                carry = jnp.concatenate([carries[half][0:r0], carry], axis=0)
            out.append(carry)
        return tuple(out)

    zero = jnp.zeros((tq, 1), F32)
    carries = (zero, zero)
    n_diag = tq // tk
    for d in range(n_diag - 1, -1, -1):
        carries = tile(i * n_diag + d, carries, d * tk, d * tk)
    carries = lax.fori_loop(
        0, i * n_diag, lambda n, cs: tile(i * n_diag - 1 - n, cs, 0, None), carries)
    o_ref[0] = jnp.where(in_half[0], acc_ref[0], acc_ref[1])


def _sb_call(qa, ka, va, tri):
    bsz, s, _ = qa.shape
    tq, tk = min(SB_Q_TILE, s), ATT_TILE
    return pl.pallas_call(
        functools.partial(_sb_body, tq=tq, tk=tk),
        grid=(bsz, D_SB // LANES, s // tq),
        in_specs=[
            pl.BlockSpec((1, tq, LANES), lambda b, hp, i: (b, i, hp)),
            pl.BlockSpec((1, s, LANES), lambda b, hp, i: (b, 0, hp)),
            pl.BlockSpec((1, s, LANES), lambda b, hp, i: (b, 0, hp)),
            pl.BlockSpec((tk, tk), lambda b, hp, i: (0, 0)),
        ],
        out_specs=pl.BlockSpec((1, tq, LANES), lambda b, hp, i: (b, i, hp)),
        out_shape=jax.ShapeDtypeStruct((bsz, s, D_SB), F32),
        scratch_shapes=[pltpu.VMEM((2, tq, LANES), F32)],
        compiler_params=pltpu.CompilerParams(
            dimension_semantics=("parallel", "parallel", "parallel"), vmem_limit_bytes=VMEM_LIMIT),
        name="stickbreak_attn",
    )(qa, ka, va, tri)


def _dsa_body(dq_ref, kv_ref, kvt_ref, iq_ref, ik_ref, iwt_ref, bias_ref, wuvt_ref, o_ref,
              key_ref, madd_ref, acc_ref, obt_ref, *, t, topk, idx_bits):
    i = pl.program_id(1)
    nblk = i + 1
    lane = lax.broadcasted_iota(I32, (1, LANES), 1)
    lo_half = lane < IDX_DIM
    srow = lax.broadcasted_iota(I32, (t, t), 0)
    qcol = lax.broadcasted_iota(I32, (t, t), 1)

    def score_keys(j):
        start = pl.multiple_of(j * t, t)
        ikb = ik_ref[0, pl.ds(start, t), :]
        zero = jnp.zeros_like(ikb)
        ik_half = (jnp.where(lo_half, ikb, zero), jnp.where(lo_half, zero, ikb))
        acc = jnp.zeros((t, t), F32)
        for hp in range(IDX_HEADS // 2):
            iq_pair = iq_ref[0, :, hp * LANES:(hp + 1) * LANES]
            for half in range(2):
                hd = 2 * hp + half
                sc = _nt(ik_half[half], iq_pair)
                acc = acc + jnp.maximum(sc, 0.0) * iwt_ref[0, hd:hd + 1, :]
        bits = lax.bitcast_convert_type(acc, I32)
        return jnp.where(bits < 0, jnp.int32(INT_MIN) - bits, bits), start

    def store_far(j, carry):
        keys, start = score_keys(j)
        key_ref[pl.ds(start, t), :] = keys
        return carry

    lax.fori_loop(0, i, store_far, 0)
    keys, start = score_keys(i)
    admissible = (srow >> 6) <= (qcol >> 6)
    key_ref[pl.ds(start, t), :] = jnp.where(admissible, keys, jnp.int32(INT_MIN))

    group = SUBLANES * COUNT_CHAINS

    def col_count(pred_fn):
        def body(j, c):
            kb = key_ref[pl.ds(pl.multiple_of(j * t, t), t), :]
            ind = pred_fn(kb, j).astype(I32)
            return c + jnp.sum(ind.reshape(t // group, group, t), axis=0)
        c = lax.fori_loop(0, nblk, body, jnp.zeros((group, t), I32))
        return jnp.sum(c, axis=0, keepdims=True)

    def value_bit(n, thr):
        cand = thr + lax.shift_left(jnp.int32(1), 31 - n)
        cnt = col_count(lambda kb, j: kb >= cand)
        return jnp.where(cnt >= topk, cand, thr)

    thr = lax.fori_loop(0, 32, value_bit, jnp.full((1, t), INT_MIN, I32))

    cnt_gt = col_count(lambda kb, j: kb > thr)
    cnt_eq = col_count(lambda kb, j: kb == thr)
    need = topk - cnt_gt
    real_thr = thr > INT_MIN
    surplus = jnp.logical_and(cnt_eq > need, real_thr)
    any_surplus = jnp.max(surplus.astype(I32)) > 0

    def index_cut():
        def index_bit(n, cut):
            cand = cut + lax.shift_left(jnp.int32(1), idx_bits - 1 - n)
            cnt = col_count(lambda kb, j: jnp.logical_and(kb == thr, srow + j * t < cand))
            return jnp.where(cnt <= need, cand, cut)
        return lax.fori_loop(0, idx_bits, index_bit, jnp.zeros((1, t), I32))

    cut = lax.cond(any_surplus, index_cut, lambda: jnp.full((1, t), 1 << idx_bits, I32))
    cut = jnp.where(real_thr, cut, 0)

    def store_mask(j, carry):
        start = pl.multiple_of(j * t, t)
        kb = key_ref[pl.ds(start, t), :]
        keep = jnp.logical_or(kb > thr, jnp.logical_and(kb == thr, srow + j * t < cut))
        madd_ref[pl.ds(start, t), :] = jnp.where(keep, 0.0, NEG_BIG)
        return carry

    lax.fori_loop(0, nblk, store_mask, 0)

    q_all = dq_ref[0].reshape(DSA_HEADS * t, DSA_KV_RANK)
    acc_ref[...] = jnp.zeros(acc_ref.shape, F32)

    def attend(j, state, near):
        m_old, l_old = state
        start = pl.multiple_of(j * t, t)
        kvb = kv_ref[0, pl.ds(start, t), :]
        madd = madd_ref[pl.ds(start, t), :]
        s_all = _nt(kvb, q_all)
        s_heads = []
        for hd in range(DSA_HEADS):
            s = s_all[:, hd * t:(hd + 1) * t] + madd
            if near:
                s = s + bias_ref[j - i + 1, hd]
            s_heads.append(s)
        s_all = jnp.concatenate(s_heads, axis=1)
        m_new = jnp.maximum(m_old, jnp.max(s_all, axis=0, keepdims=True))
        alpha = jnp.exp2(m_old - m_new)
        p = jnp.exp2(s_all - m_new)
        l_new = l_old * alpha + jnp.sum(p, axis=0, keepdims=True)
        acc_ref[...] = acc_ref[...] * alpha + _dot(kvt_ref[0, j], p.astype(BF16))
        return m_new, l_new

    state = (jnp.full((1, DSA_HEADS * t), NEG_BIG, F32), jnp.zeros((1, DSA_HEADS * t), F32))
    n_far = jnp.maximum(i - 1, 0)
    state = lax.fori_loop(0, n_far, lambda j, st: attend(j, st, False), state)
    _, l_fin = lax.fori_loop(n_far, nblk, lambda j, st: attend(j, st, True), state)

    o_t = (acc_ref[...] * (1.0 / l_fin)).astype(BF16)
    for hd in range(DSA_HEADS):
        obt_ref[hd * DSA_HEAD_DIM:(hd + 1) * DSA_HEAD_DIM, :] = _dot(
            wuvt_ref[hd], o_t[:, hd * t:(hd + 1) * t])
    o_ref[0] = obt_ref[...].T


def _dsa_call(dq, kv, kvt, iq, ik2, iwt, bias, wuvt):
    bsz, s, _ = kv.shape
    t = ATT_TILE
    nkb = s // t
    topk = min(DSA_MAX_TOPK, s // 4)
    body = functools.partial(_dsa_body, t=t, topk=topk, idx_bits=int(s).bit_length())
    return pl.pallas_call(
        body,
        grid=(bsz, nkb),
        in_specs=[
            pl.BlockSpec((1, DSA_HEADS, t, DSA_KV_RANK), lambda b, i: (b, 0, i, 0)),
            pl.BlockSpec((1, s, DSA_KV_RANK), lambda b, i: (b, 0, 0)),
            pl.BlockSpec((1, nkb, DSA_KV_RANK, t), lambda b, i: (b, 0, 0, 0)),
            pl.BlockSpec((1, t, D_IDX_Q), lambda b, i: (b, i, 0)),
            pl.BlockSpec((1, s, LANES), lambda b, i: (b, 0, 0)),
            pl.BlockSpec((1, IDX_HEADS, t), lambda b, i: (b, 0, i)),
            pl.BlockSpec((2, DSA_HEADS, t, t), lambda b, i: (0, 0, 0, 0)),
            pl.BlockSpec((DSA_HEADS, DSA_HEAD_DIM, DSA_KV_RANK), lambda b, i: (0, 0, 0)),
        ],
        out_specs=pl.BlockSpec((1, t, D_DSA_OUT), lambda b, i: (b, i, 0)),
        out_shape=jax.ShapeDtypeStruct((bsz, s, D_DSA_OUT), F32),
        scratch_shapes=[
            pltpu.VMEM((s, t), I32),
            pltpu.VMEM((s, t), F32),
            pltpu.VMEM((DSA_KV_RANK, DSA_HEADS * t), F32),
            pltpu.VMEM((D_DSA_OUT, t), F32),
        ],
        compiler_params=pltpu.CompilerParams(
            dimension_semantics=("parallel", "parallel"), vmem_limit_bytes=VMEM_LIMIT),
        name="sparse_attn",
    )(dq, kv, kvt, iq, ik2, iwt, bias, wuvt)


def _post_body(oa_ref, ob_ref, x_ref, mod_ref, ga_ref, gb_ref, wout_ref, gm_ref, wup_ref, wdn_ref,
               gf_ref, o_ref, *, final):
    ga1 = mod_ref[0, 2:3, :]
    sh2 = mod_ref[0, 3:4, :]
    sc2 = mod_ref[0, 4:5, :]
    ga2 = mod_ref[0, 5:6, :]
    na = (_rms(oa_ref[0]) * ga_ref[...]).astype(BF16)
    nb = (_rms(ob_ref[0]) * gb_ref[...]).astype(BF16)
    attn = _dot(na, wout_ref[0:D_SB, :]) + _dot(nb, wout_ref[D_SB:, :])
    x1 = x_ref[0] + ga1 * attn
    h = ((_rms(x1) * gm_ref[...]) * (1.0 + sc2) + sh2).astype(BF16)
    acc = jnp.zeros_like(x1)
    for cblk in range(D_FF // FF_TILE):
        u = _dot(h, wup_ref[:, cblk * FF_TILE:(cblk + 1) * FF_TILE])
        r = jnp.maximum(u, 0.0)
        acc = acc + _dot((r * r).astype(BF16), wdn_ref[cblk * FF_TILE:(cblk + 1) * FF_TILE, :])
    x2 = x1 + ga2 * acc
    if final:
        x2 = _rms(x2) * gf_ref[...]
    o_ref[0] = x2


def _post_call(o_a, o_b, x, modl, g_a, g_b, w_out, g_mlp, w_up, w_down, g_final, final):
    bsz, s, d = x.shape
    tm = min(TOKEN_TILE, s)
    const2 = lambda b, i: (0, 0)
    return pl.pallas_call(
        functools.partial(_post_body, final=final),
        grid=(bsz, s // tm),
        in_specs=[
            pl.BlockSpec((1, tm, D_SB), lambda b, i: (b, i, 0)),
            pl.BlockSpec((1, tm, D_DSA_OUT), lambda b, i: (b, i, 0)),
            pl.BlockSpec((1, tm, d), lambda b, i: (b, i, 0)),
            pl.BlockSpec((1, 6, d), lambda b, i: (b, 0, 0)),
            pl.BlockSpec((1, D_SB), const2),
            pl.BlockSpec((1, D_DSA_OUT), const2),
            pl.BlockSpec((d, d), const2),
            pl.BlockSpec((1, d), const2),
            pl.BlockSpec((d, D_FF), const2),
            pl.BlockSpec((D_FF, d), const2),
            pl.BlockSpec((1, d), const2),
        ],
        out_specs=pl.BlockSpec((1, tm, d), lambda b, i: (b, i, 0)),
        out_shape=jax.ShapeDtypeStruct((bsz, s, d), F32),
        compiler_params=pltpu.CompilerParams(
            dimension_semantics=("parallel", "parallel"), vmem_limit_bytes=VMEM_LIMIT),
        name="outproj_mlp",
    )(o_a, o_b, x, modl, g_a.reshape(1, -1), g_b.reshape(1, -1), w_out, g_mlp.reshape(1, d),
      w_up, w_down, g_final.reshape(1, d))


def _t5_bucket_table(rel):
    nb = N_BUCKETS // 2
    max_exact = nb // 2
    n = np.abs(rel)
    nf = np.maximum(n, max_exact).astype(np.float64)
    large = max_exact + (np.log(nf / max_exact) / math.log(MAX_DISTANCE / max_exact)
                         * (nb - max_exact)).astype(np.int64)
    large = np.minimum(large, nb - 1)
    return np.where(rel > 0, nb, 0) + np.where(n < max_exact, n, large)


def _toeplitz_key_query(vec, t):
    width = 2 * t
    ext = jnp.pad(vec, ((0, 0), (0, 1)))
    flat = jnp.tile(ext, (1, t))[:, :t * (width - 1)]
    by_query = flat.reshape(vec.shape[0], t, width - 1)[:, :, t - 1:]
    return jnp.swapaxes(by_query, 1, 2)


def _bias_tiles(rel_bias, t, s):
    far_bucket = _t5_bucket_table(np.arange(-(s - 1), -t + 1))
    assert t >= MAX_DISTANCE and np.all(far_bucket == far_bucket[0])
    offs = np.arange(-(t - 1), t)
    tab = rel_bias.astype(F32)
    far = tab[int(far_bucket[0])]
    vec_diag = (tab[_t5_bucket_table(offs)] - far).T * LOG2E
    vec_prev = (tab[_t5_bucket_table(offs - t)] - far).T * LOG2E
    return jnp.stack([_toeplitz_key_query(vec_prev, t), _toeplitz_key_query(vec_diag, t)])


def _cat_weight(w_in_l):
    c_sb = 3 * D_SB
    c_dq = c_sb + D_DSA_Q
    c_kv = c_dq + DSA_KV_RANK
    c_iq = c_kv + D_IDX_Q
    c_ik = c_iq + IDX_DIM
    ik = w_in_l[:, c_iq:c_ik]
    iw = w_in_l[:, c_ik:]
    pad = jnp.zeros((w_in_l.shape[0], LANES - IDX_HEADS), w_in_l.dtype)
    return jnp.concatenate([w_in_l[:, :c_iq], ik, ik, iw, pad], axis=1).astype(BF16)


def kernel(x, c, w_mod, b_mod, g_attn, w_in, kv_norm_g, w_uv, g_out_a, g_out_b, w_out, g_mlp,
           w_up, w_down, rel_bias, g_final):
    bsz, s, d = x.shape
    assert d == D_MODEL and s % TOKEN_TILE == 0 and TOKEN_TILE % ATT_TILE == 0
    mod = _mod_call(c, w_mod, b_mod)
    bias = _bias_tiles(rel_bias, ATT_TILE, s)
    tri = jnp.asarray(np.tril(np.ones((ATT_TILE, ATT_TILE), np.float32), -1), BF16)
    for l in range(DEPTH):
        modl = mod[l].reshape(bsz, 6, d)
        qa, ka, va, dq, kv, kvt, iq, ik2, iwt = _inproj_call(
            x, modl, g_attn[l], _cat_weight(w_in[l]), kv_norm_g[l])
        o_a = _sb_call(qa, ka, va, tri)
        wuvt = jnp.transpose(w_uv[l], (0, 2, 1)).astype(BF16)
        o_b = _dsa_call(dq, kv, kvt, iq, ik2, iwt, bias, wuvt)
        x = _post_call(o_a, o_b, x, modl, g_out_a[l], g_out_b[l], w_out[l].astype(BF16), g_mlp[l],
                       w_up[l].astype(BF16), w_down[l].astype(BF16), g_final, l == DEPTH - 1)
    return x
```

```python
import functools
import math

import numpy as np
import jax
import jax.numpy as jnp
from jax import lax
from jax.experimental import pallas as pl
from jax.experimental.pallas import tpu as pltpu

F32 = jnp.float32
BF16 = jnp.bfloat16
I32 = jnp.int32
I16 = jnp.int16

D_MODEL = 1024
DEPTH = 2
CHUNK = 64
SB_HEADS = 8
SB_HEAD_DIM = 64
DSA_HEADS = 8
DSA_HEAD_DIM = 64
DSA_KV_RANK = 128
IDX_HEADS = 8
IDX_DIM = 64
DSA_MAX_TOPK = 256
D_FF = 4 * D_MODEL
N_BUCKETS = 32
MAX_DISTANCE = 128
EPS = 1e-6

D_SB = SB_HEADS * SB_HEAD_DIM
D_DSA_Q = DSA_HEADS * DSA_KV_RANK
D_IDX_Q = IDX_HEADS * IDX_DIM
D_DSA_OUT = DSA_HEADS * DSA_HEAD_DIM

LANES = 128
SUBLANES = 8
ATT_TILE = 256
SB_Q_TILE = 2 * ATT_TILE
COUNT_CHAINS = 4
TOKEN_TILE = 512
FF_TILE = 1024
MOD_TILE = 1024
VMEM_LIMIT = 56 * 1024 * 1024

NEG_BIG = -1e30
INT_MIN = -2 ** 31
HALF_BIAS = 2 ** 15
LOG2E = 1.4426950408889634

C_QA, C_KA, C_VA = 0, D_SB, 2 * D_SB
C_DQ = 3 * D_SB
C_KV = C_DQ + D_DSA_Q
C_IQ = C_KV + DSA_KV_RANK
C_IK = C_IQ + D_IDX_Q
C_IW = C_IK + LANES
D_CAT = C_IW + LANES


def _nt(a, b):
    return lax.dot_general(a, b, (((1,), (1,)), ((), ())), preferred_element_type=F32)


def _dot(a, b):
    return jnp.dot(a, b, preferred_element_type=F32)


def _rms(v, axis=-1):
    return v * lax.rsqrt(jnp.mean(v * v, axis=axis, keepdims=True) + EPS)


def _mod_body(c_ref, w_ref, b_ref, o_ref):
    c = c_ref[...]
    ca = c / (1.0 + jnp.exp(-c))
    c_hi = ca.astype(BF16)
    c_lo = (ca - c_hi.astype(F32)).astype(BF16)
    w = w_ref[0]
    w_hi = w.astype(BF16)
    w_lo = (w - w_hi.astype(F32)).astype(BF16)
    o_ref[0] = _dot(c_hi, w_hi) + _dot(c_lo, w_hi) + _dot(c_hi, w_lo) + b_ref[0]


def _mod_call(c, w_mod, b_mod):
    depth, d, n = w_mod.shape
    bsz = c.shape[0]
    return pl.pallas_call(
        _mod_body,
        grid=(depth, n // MOD_TILE),
        in_specs=[
            pl.BlockSpec((bsz, d), lambda l, j: (0, 0)),
            pl.BlockSpec((1, d, MOD_TILE), lambda l, j: (l, 0, j)),
            pl.BlockSpec((1, 1, MOD_TILE), lambda l, j: (l, 0, j)),
        ],
        out_specs=pl.BlockSpec((1, bsz, MOD_TILE), lambda l, j: (l, 0, j)),
        out_shape=jax.ShapeDtypeStruct((depth, bsz, n), F32),
        compiler_params=pltpu.CompilerParams(vmem_limit_bytes=VMEM_LIMIT),
        name="adaln_mod",
    )(c, w_mod, b_mod.reshape(depth, 1, n))


def _inproj_body(x_ref, mod_ref, g_ref, w_ref, kvg_ref,
                 qa_ref, ka_ref, va_ref, dq_ref, kv_ref, kvt_ref, iq_ref, ik_ref, iwt_ref):
    x = x_ref[0]
    sh1 = mod_ref[0, 0:1, :]
    sc1 = mod_ref[0, 1:2, :]
    h = (_rms(x) * g_ref[...]) * (1.0 + sc1) + sh1
    h = h.astype(BF16)

    def proj(c0, width):
        return _dot(h, w_ref[:, c0:c0 + width])

    qa_ref[0] = (proj(C_QA, D_SB) * (SB_HEAD_DIM ** -0.5 * LOG2E)).astype(BF16)
    ka_ref[0] = proj(C_KA, D_SB).astype(BF16)
    va_ref[0] = proj(C_VA, D_SB).astype(BF16)
    for hd in range(DSA_HEADS):
        dq_ref[0, hd] = (proj(C_DQ + hd * DSA_KV_RANK, DSA_KV_RANK)
                         * (DSA_KV_RANK ** -0.5 * LOG2E)).astype(BF16)
    kv = _rms(proj(C_KV, DSA_KV_RANK)) * kvg_ref[...]
    kv_ref[0] = kv.astype(BF16)
    kv_t = kv.T.astype(BF16)
    for cblk in range(kvt_ref.shape[1]):
        kvt_ref[0, cblk] = kv_t[:, cblk * ATT_TILE:(cblk + 1) * ATT_TILE]
    for hp in range(IDX_HEADS // 2):
        iq_ref[0, hp] = proj(C_IQ + hp * LANES, LANES).astype(BF16)
    ik_ref[0] = proj(C_IK, LANES).astype(BF16)
    iwt_ref[0] = proj(C_IW, LANES).T[0:IDX_HEADS, :]


def _inproj_call(x, modl, g, wcat, kvg):
    bsz, s, d = x.shape
    tm = min(TOKEN_TILE, s)
    nkb = tm // ATT_TILE
    out_shapes = (
        jax.ShapeDtypeStruct((bsz, s, D_SB), BF16),
        jax.ShapeDtypeStruct((bsz, s, D_SB), BF16),
        jax.ShapeDtypeStruct((bsz, s, D_SB), BF16),
        jax.ShapeDtypeStruct((bsz, DSA_HEADS, s, DSA_KV_RANK), BF16),
        jax.ShapeDtypeStruct((bsz, s, DSA_KV_RANK), BF16),
        jax.ShapeDtypeStruct((bsz, s // ATT_TILE, DSA_KV_RANK, ATT_TILE), BF16),
        jax.ShapeDtypeStruct((bsz, IDX_HEADS // 2, s, LANES), BF16),
        jax.ShapeDtypeStruct((bsz, s, LANES), BF16),
        jax.ShapeDtypeStruct((bsz, IDX_HEADS, s), F32),
    )
    out_specs = (
        pl.BlockSpec((1, tm, D_SB), lambda b, i: (b, i, 0)),
        pl.BlockSpec((1, tm, D_SB), lambda b, i: (b, i, 0)),
        pl.BlockSpec((1, tm, D_SB), lambda b, i: (b, i, 0)),
        pl.BlockSpec((1, DSA_HEADS, tm, DSA_KV_RANK), lambda b, i: (b, 0, i, 0)),
        pl.BlockSpec((1, tm, DSA_KV_RANK), lambda b, i: (b, i, 0)),
        pl.BlockSpec((1, nkb, DSA_KV_RANK, ATT_TILE), lambda b, i: (b, i, 0, 0)),
        pl.BlockSpec((1, IDX_HEADS // 2, tm, LANES), lambda b, i: (b, 0, i, 0)),
        pl.BlockSpec((1, tm, LANES), lambda b, i: (b, i, 0)),
        pl.BlockSpec((1, IDX_HEADS, tm), lambda b, i: (b, 0, i)),
    )
    return pl.pallas_call(
        _inproj_body,
        grid=(bsz, s // tm),
        in_specs=[
            pl.BlockSpec((1, tm, d), lambda b, i: (b, i, 0)),
            pl.BlockSpec((1, 6, d), lambda b, i: (b, 0, 0)),
            pl.BlockSpec((1, d), lambda b, i: (0, 0)),
            pl.BlockSpec((d, D_CAT), lambda b, i: (0, 0)),
            pl.BlockSpec((1, DSA_KV_RANK), lambda b, i: (0, 0)),
        ],
        out_specs=out_specs,
        out_shape=out_shapes,
        compiler_params=pltpu.CompilerParams(
            dimension_semantics=("parallel", "parallel"), vmem_limit_bytes=VMEM_LIMIT),
        name="inproj",
    )(x, modl, g.reshape(1, d), wcat, kvg.reshape(1, DSA_KV_RANK))


def _sb_body(q_ref, k_ref, v_ref, tri_ref, o_ref, acc_ref, *, tq, tk):
    i = pl.program_id(2)
    lane = lax.broadcasted_iota(I32, (1, LANES), 1)
    in_half = (lane < SB_HEAD_DIM, lane >= SB_HEAD_DIM)
    tri = tri_ref[...]
    acc_ref[...] = jnp.zeros(acc_ref.shape, F32)

    def tile(j, carries, r0, diag_off):
        start = pl.multiple_of(j * tk, tk)
        kb = k_ref[0, pl.ds(start, tk), :]
        vb = v_ref[0, pl.ds(start, tk), :]
        q2 = q_ref[0, r0:tq, :]
        if diag_off is not None:
            row = lax.broadcasted_iota(I32, (tq - r0, tk), 0) + r0
            col = lax.broadcasted_iota(I32, (tq - r0, tk), 1) + diag_off
            causal = col < row
        out = []
        for half in range(2):
            kh = jnp.where(in_half[half], kb, jnp.zeros_like(kb))
            z = _nt(q2, kh)
            neg_abs = lax.bitcast_convert_type(
                lax.bitcast_convert_type(z, I32) | jnp.int32(INT_MIN), F32)
            sp = jnp.log(1.0 + jnp.exp2(neg_abs)) * LOG2E
            log_beta = jnp.minimum(z, 0.0) - sp
            log_1m = log_beta - z
            if diag_off is not None:
                log_1m = jnp.where(causal, log_1m, 0.0)
            after = _dot(log_1m.astype(BF16), tri)
            carry = carries[half][r0:tq]
            w = jnp.exp2(log_beta + after + carry)
            if diag_off is not None:
                w = jnp.where(causal, w, 0.0)
            acc_ref[half, r0:tq, :] += _dot(w.astype(BF16), vb)
            carry = carry + jnp.sum(log_1m, axis=-1, keepdims=True)
            if r0 > 0:
                carry = jnp.concatenate([carries[half][0:r0], carry], axis=0)
            out.append(carry)
        return tuple(out)

    zero = jnp.zeros((tq, 1), F32)
    carries = (zero, zero)
    n_diag = tq // tk
    for d in range(n_diag - 1, -1, -1):
        carries = tile(i * n_diag + d, carries, d * tk, d * tk)
    def far_group(n, cs):
        for d in range(n_diag):
            cs = tile((i - n) * n_diag - 1 - d, cs, 0, None)
        return cs

    carries = lax.fori_loop(0, i, far_group, carries)
    o_ref[0] = jnp.where(in_half[0], acc_ref[0], acc_ref[1])


def _sb_call(qa, ka, va, tri):
    bsz, s, _ = qa.shape
    tq, tk = min(SB_Q_TILE, s), ATT_TILE
    return pl.pallas_call(
        functools.partial(_sb_body, tq=tq, tk=tk),
        grid=(bsz, D_SB // LANES, s // tq),
        in_specs=[
            pl.BlockSpec((1, tq, LANES), lambda b, hp, i: (b, i, hp)),
            pl.BlockSpec((1, s, LANES), lambda b, hp, i: (b, 0, hp)),
            pl.BlockSpec((1, s, LANES), lambda b, hp, i: (b, 0, hp)),
            pl.BlockSpec((tk, tk), lambda b, hp, i: (0, 0)),
        ],
        out_specs=pl.BlockSpec((1, tq, LANES), lambda b, hp, i: (b, i, hp)),
        out_shape=jax.ShapeDtypeStruct((bsz, s, D_SB), F32),
        scratch_shapes=[pltpu.VMEM((2, tq, LANES), F32)],
        compiler_params=pltpu.CompilerParams(
            dimension_semantics=("parallel", "parallel", "parallel"), vmem_limit_bytes=VMEM_LIMIT),
        name="stickbreak_attn",
    )(qa, ka, va, tri)


def _dsa_body(dq_ref, kv_ref, kvt_ref, iq_ref, ik_ref, iwt_ref, bias_ref, wuvt_ref, o_ref,
              key_ref, hi_ref, lo_ref, madd_ref, acc_ref, obt_ref, *, t, topk, idx_bits):
    i = pl.program_id(1)
    nblk = i + 1
    lane = lax.broadcasted_iota(I32, (1, LANES), 1)
    lo_half = lane < IDX_DIM
    srow = lax.broadcasted_iota(I32, (t, t), 0)
    qcol = lax.broadcasted_iota(I32, (t, t), 1)

    iq_pairs = iq_ref[0].reshape((IDX_HEADS // 2) * t, LANES)

    def score_keys(j):
        start = pl.multiple_of(j * t, t)
        ikb = ik_ref[0, pl.ds(start, t), :]
        zero = jnp.zeros_like(ikb)
        sc_half = (_nt(jnp.where(lo_half, ikb, zero), iq_pairs),
                   _nt(jnp.where(lo_half, zero, ikb), iq_pairs))
        acc = jnp.zeros((t, t), F32)
        for hd in range(IDX_HEADS):
            sc = sc_half[hd % 2][:, (hd // 2) * t:(hd // 2 + 1) * t]
            acc = acc + jnp.maximum(sc, 0.0) * iwt_ref[0, hd:hd + 1, :]
        bits = lax.bitcast_convert_type(acc, I32)
        return jnp.where(bits < 0, jnp.int32(INT_MIN) - bits, bits), start

    def store_keys(keys, start):
        key_ref[pl.ds(start, t), :] = keys
        hi_ref[pl.ds(start, t), :] = (keys >> 16).astype(I16)
        lo_ref[pl.ds(start, t), :] = ((keys & 0xFFFF) - HALF_BIAS).astype(I16)

    def store_far(j, carry):
        store_keys(*score_keys(j))
        return carry

    lax.fori_loop(0, i, store_far, 0)
    keys, start = score_keys(i)
    admissible = (srow >> 6) <= (qcol >> 6)
    store_keys(jnp.where(admissible, keys, jnp.int32(INT_MIN)), start)

    group = SUBLANES * COUNT_CHAINS
    group16 = 2 * group

    def col_count(pred_fn):
        def body(j, c):
            kb = key_ref[pl.ds(pl.multiple_of(j * t, t), t), :]
            ind = pred_fn(kb, j).astype(I32)
            return c + jnp.sum(ind.reshape(t // group, group, t), axis=0)
        c = lax.fori_loop(0, nblk, body, jnp.zeros((group, t), I32))
        return jnp.sum(c, axis=0, keepdims=True)

    def col_count16(ref, cand):
        def body(j, c):
            blk = ref[pl.ds(pl.multiple_of(j * t, t), t), :]
            ind = jnp.where(blk >= cand, jnp.int16(1), jnp.int16(0))
            for r in range(t // group16):
                c = c + ind[r * group16:(r + 1) * group16, :]
            return c
        c = lax.fori_loop(0, nblk, body, jnp.zeros((group16, t), I16))
        return jnp.sum(c.astype(I32), axis=0, keepdims=True)

    def half_search(ref, want):
        def bit_step(n, thr):
            cand = thr + lax.shift_left(jnp.int32(1), 15 - n)
            cnt = col_count16(ref, cand.astype(I16))
            return jnp.where(cnt >= want, cand, thr)
        return lax.fori_loop(0, 16, bit_step, jnp.full((1, t), -HALF_BIAS, I32))

    thr_hi = half_search(hi_ref, topk)
    above = col_count16(hi_ref, (thr_hi + 1).astype(I16))
    above = jnp.where(thr_hi == HALF_BIAS - 1, 0, above)
    thr_hi16 = thr_hi.astype(I16)

    def keep_bucket(j, carry):
        rows = pl.ds(pl.multiple_of(j * t, t), t)
        lo_ref[rows, :] = jnp.where(hi_ref[rows, :] == thr_hi16, lo_ref[rows, :], jnp.int16(-HALF_BIAS))
        return carry

    lax.fori_loop(0, nblk, keep_bucket, 0)
    thr_lo = half_search(lo_ref, topk - above)
    thr = (thr_hi << 16) | (thr_lo + HALF_BIAS)

    cnt_gt = col_count(lambda kb, j: kb > thr)
    cnt_eq = col_count(lambda kb, j: kb == thr)
    need = topk - cnt_gt
    real_thr = thr > INT_MIN
    surplus = jnp.logical_and(cnt_eq > need, real_thr)
    any_surplus = jnp.max(surplus.astype(I32)) > 0

    def index_cut():
        def index_bit(n, cut):
            cand = cut + lax.shift_left(jnp.int32(1), idx_bits - 1 - n)
            cnt = col_count(lambda kb, j: jnp.logical_and(kb == thr, srow + j * t < cand))
            return jnp.where(cnt <= need, cand, cut)
        return lax.fori_loop(0, idx_bits, index_bit, jnp.zeros((1, t), I32))

    cut = lax.cond(any_surplus, index_cut, lambda: jnp.full((1, t), 1 << idx_bits, I32))
    cut = jnp.where(real_thr, cut, 0)

    def store_mask(j, carry):
        start = pl.multiple_of(j * t, t)
        kb = key_ref[pl.ds(start, t), :]
        keep = jnp.logical_or(kb > thr, jnp.logical_and(kb == thr, srow + j * t < cut))
        madd_ref[pl.ds(start, t), :] = jnp.where(keep, 0.0, NEG_BIG)
        return carry

    lax.fori_loop(0, nblk, store_mask, 0)

    q_all = dq_ref[0].reshape(DSA_HEADS * t, DSA_KV_RANK)
    acc_ref[...] = jnp.zeros(acc_ref.shape, F32)

    def attend(j, state, near):
        m_old, l_old = state
        start = pl.multiple_of(j * t, t)
        kvb = kv_ref[0, pl.ds(start, t), :]
        madd = madd_ref[pl.ds(start, t), :]
        s_all = _nt(kvb, q_all)
        s_heads = []
        for hd in range(DSA_HEADS):
            s = s_all[:, hd * t:(hd + 1) * t] + madd
            if near:
                s = s + bias_ref[j - i + 1, hd]
            s_heads.append(s)
        s_all = jnp.concatenate(s_heads, axis=1)
        m_new = jnp.maximum(m_old, jnp.max(s_all, axis=0, keepdims=True))
        alpha = jnp.exp2(m_old - m_new)
        p = jnp.exp2(s_all - m_new)
        l_new = l_old * alpha + jnp.sum(p, axis=0, keepdims=True)
        acc_ref[...] = acc_ref[...] * alpha + _dot(kvt_ref[0, j], p.astype(BF16))
        return m_new, l_new

    state = (jnp.full((1, DSA_HEADS * t), NEG_BIG, F32), jnp.zeros((1, DSA_HEADS * t), F32))
    n_far = jnp.maximum(i - 1, 0)
    state = lax.fori_loop(0, n_far, lambda j, st: attend(j, st, False), state)
    _, l_fin = lax.fori_loop(n_far, nblk, lambda j, st: attend(j, st, True), state)

    o_t = (acc_ref[...] * (1.0 / l_fin)).astype(BF16)
    for hd in range(DSA_HEADS):
        obt_ref[hd * DSA_HEAD_DIM:(hd + 1) * DSA_HEAD_DIM, :] = _dot(
            wuvt_ref[hd], o_t[:, hd * t:(hd + 1) * t])
    o_ref[0] = obt_ref[...].T


def _dsa_call(dq, kv, kvt, iq, ik2, iwt, bias, wuvt):
    bsz, s, _ = kv.shape
    t = ATT_TILE
    nkb = s // t
    topk = min(DSA_MAX_TOPK, s // 4)
    body = functools.partial(_dsa_body, t=t, topk=topk, idx_bits=int(s).bit_length())
    return pl.pallas_call(
        body,
        grid=(bsz, nkb),
        in_specs=[
            pl.BlockSpec((1, DSA_HEADS, t, DSA_KV_RANK), lambda b, i: (b, 0, i, 0)),
            pl.BlockSpec((1, s, DSA_KV_RANK), lambda b, i: (b, 0, 0)),
            pl.BlockSpec((1, nkb, DSA_KV_RANK, t), lambda b, i: (b, 0, 0, 0)),
            pl.BlockSpec((1, IDX_HEADS // 2, t, LANES), lambda b, i: (b, 0, i, 0)),
            pl.BlockSpec((1, s, LANES), lambda b, i: (b, 0, 0)),
            pl.BlockSpec((1, IDX_HEADS, t), lambda b, i: (b, 0, i)),
            pl.BlockSpec((2, DSA_HEADS, t, t), lambda b, i: (0, 0, 0, 0)),
            pl.BlockSpec((DSA_HEADS, DSA_HEAD_DIM, DSA_KV_RANK), lambda b, i: (0, 0, 0)),
        ],
        out_specs=pl.BlockSpec((1, t, D_DSA_OUT), lambda b, i: (b, i, 0)),
        out_shape=jax.ShapeDtypeStruct((bsz, s, D_DSA_OUT), F32),
        scratch_shapes=[
            pltpu.VMEM((s, t), I32),
            pltpu.VMEM((s, t), I16),
            pltpu.VMEM((s, t), I16),
            pltpu.VMEM((s, t), F32),
            pltpu.VMEM((DSA_KV_RANK, DSA_HEADS * t), F32),
            pltpu.VMEM((D_DSA_OUT, t), F32),
        ],
        compiler_params=pltpu.CompilerParams(
            dimension_semantics=("parallel", "parallel"), vmem_limit_bytes=VMEM_LIMIT),
        name="sparse_attn",
    )(dq, kv, kvt, iq, ik2, iwt, bias, wuvt)


def _post_body(oa_ref, ob_ref, x_ref, mod_ref, ga_ref, gb_ref, wout_ref, gm_ref, wup_ref, wdn_ref,
               gf_ref, o_ref, *, final):
    ga1 = mod_ref[0, 2:3, :]
    sh2 = mod_ref[0, 3:4, :]
    sc2 = mod_ref[0, 4:5, :]
    ga2 = mod_ref[0, 5:6, :]
    na = (_rms(oa_ref[0]) * ga_ref[...]).astype(BF16)
    nb = (_rms(ob_ref[0]) * gb_ref[...]).astype(BF16)
    attn = _dot(na, wout_ref[0:D_SB, :]) + _dot(nb, wout_ref[D_SB:, :])
    x1 = x_ref[0] + ga1 * attn
    h = ((_rms(x1) * gm_ref[...]) * (1.0 + sc2) + sh2).astype(BF16)
    acc = jnp.zeros_like(x1)
    for cblk in range(D_FF // FF_TILE):
        u = _dot(h, wup_ref[:, cblk * FF_TILE:(cblk + 1) * FF_TILE])
        r = jnp.maximum(u, 0.0)
        acc = acc + _dot((r * r).astype(BF16), wdn_ref[cblk * FF_TILE:(cblk + 1) * FF_TILE, :])
    x2 = x1 + ga2 * acc
    if final:
        x2 = _rms(x2) * gf_ref[...]
    o_ref[0] = x2


def _post_call(o_a, o_b, x, modl, g_a, g_b, w_out, g_mlp, w_up, w_down, g_final, final):
    bsz, s, d = x.shape
    tm = min(TOKEN_TILE, s)
    const2 = lambda b, i: (0, 0)
    return pl.pallas_call(
        functools.partial(_post_body, final=final),
        grid=(bsz, s // tm),
        in_specs=[
            pl.BlockSpec((1, tm, D_SB), lambda b, i: (b, i, 0)),
            pl.BlockSpec((1, tm, D_DSA_OUT), lambda b, i: (b, i, 0)),
            pl.BlockSpec((1, tm, d), lambda b, i: (b, i, 0)),
            pl.BlockSpec((1, 6, d), lambda b, i: (b, 0, 0)),
            pl.BlockSpec((1, D_SB), const2),
            pl.BlockSpec((1, D_DSA_OUT), const2),
            pl.BlockSpec((d, d), const2),
            pl.BlockSpec((1, d), const2),
            pl.BlockSpec((d, D_FF), const2),
            pl.BlockSpec((D_FF, d), const2),
            pl.BlockSpec((1, d), const2),
        ],
        out_specs=pl.BlockSpec((1, tm, d), lambda b, i: (b, i, 0)),
        out_shape=jax.ShapeDtypeStruct((bsz, s, d), F32),
        compiler_params=pltpu.CompilerParams(
            dimension_semantics=("parallel", "parallel"), vmem_limit_bytes=VMEM_LIMIT),
        name="outproj_mlp",
    )(o_a, o_b, x, modl, g_a.reshape(1, -1), g_b.reshape(1, -1), w_out, g_mlp.reshape(1, d),
      w_up, w_down, g_final.reshape(1, d))


def _t5_bucket_table(rel):
    nb = N_BUCKETS // 2
    max_exact = nb // 2
    n = np.abs(rel)
    nf = np.maximum(n, max_exact).astype(np.float64)
    large = max_exact + (np.log(nf / max_exact) / math.log(MAX_DISTANCE / max_exact)
                         * (nb - max_exact)).astype(np.int64)
    large = np.minimum(large, nb - 1)
    return np.where(rel > 0, nb, 0) + np.where(n < max_exact, n, large)


def _toeplitz_key_query(vec, t):
    width = 2 * t
    ext = jnp.pad(vec, ((0, 0), (0, 1)))
    flat = jnp.tile(ext, (1, t))[:, :t * (width - 1)]
    by_query = flat.reshape(vec.shape[0], t, width - 1)[:, :, t - 1:]
    return jnp.swapaxes(by_query, 1, 2)


def _bias_tiles(rel_bias, t, s):
    far_bucket = _t5_bucket_table(np.arange(-(s - 1), -t + 1))
    assert t >= MAX_DISTANCE and np.all(far_bucket == far_bucket[0])
    offs = np.arange(-(t - 1), t)
    tab = rel_bias.astype(F32)
    far = tab[int(far_bucket[0])]
    vec_diag = (tab[_t5_bucket_table(offs)] - far).T * LOG2E
    vec_prev = (tab[_t5_bucket_table(offs - t)] - far).T * LOG2E
    return jnp.stack([_toeplitz_key_query(vec_prev, t), _toeplitz_key_query(vec_diag, t)])


def _cat_weight(w_in_l):
    c_sb = 3 * D_SB
    c_dq = c_sb + D_DSA_Q
    c_kv = c_dq + DSA_KV_RANK
    c_iq = c_kv + D_IDX_Q
    c_ik = c_iq + IDX_DIM
    ik = w_in_l[:, c_iq:c_ik]
    iw = w_in_l[:, c_ik:]
    pad = jnp.zeros((w_in_l.shape[0], LANES - IDX_HEADS), w_in_l.dtype)
    return jnp.concatenate([w_in_l[:, :c_iq], ik, ik, iw, pad], axis=1).astype(BF16)


def kernel(x, c, w_mod, b_mod, g_attn, w_in, kv_norm_g, w_uv, g_out_a, g_out_b, w_out, g_mlp,
           w_up, w_down, rel_bias, g_final):
    bsz, s, d = x.shape
    assert d == D_MODEL and s % TOKEN_TILE == 0 and TOKEN_TILE % ATT_TILE == 0
    mod = _mod_call(c, w_mod, b_mod)
    bias = _bias_tiles(rel_bias, ATT_TILE, s)
    tri = jnp.asarray(np.tril(np.ones((ATT_TILE, ATT_TILE), np.float32), -1), BF16)
    for l in range(DEPTH):
        modl = mod[l].reshape(bsz, 6, d)
        qa, ka, va, dq, kv, kvt, iq, ik2, iwt = _inproj_call(
            x, modl, g_attn[l], _cat_weight(w_in[l]), kv_norm_g[l])
        o_a = _sb_call(qa, ka, va, tri)
        wuvt = jnp.transpose(w_uv[l], (0, 2, 1)).astype(BF16)
        o_b = _dsa_call(dq, kv, kvt, iq, ik2, iwt, bias, wuvt)
        x = _post_call(o_a, o_b, x, modl, g_out_a[l], g_out_b[l], w_out[l].astype(BF16), g_mlp[l],
                       w_up[l].astype(BF16), w_down[l].astype(BF16), g_final, l == DEPTH - 1)
    return x
```

```python
import functools
import math

import numpy as np
import jax
import jax.numpy as jnp
from jax import lax
from jax.experimental import pallas as pl
from jax.experimental.pallas import tpu as pltpu

F32 = jnp.float32
BF16 = jnp.bfloat16
I32 = jnp.int32
I16 = jnp.int16

D_MODEL = 1024
DEPTH = 2
CHUNK = 64
SB_HEADS = 8
SB_HEAD_DIM = 64
DSA_HEADS = 8
DSA_HEAD_DIM = 64
DSA_KV_RANK = 128
IDX_HEADS = 8
IDX_DIM = 64
DSA_MAX_TOPK = 256
D_FF = 4 * D_MODEL
N_BUCKETS = 32
MAX_DISTANCE = 128
EPS = 1e-6

D_SB = SB_HEADS * SB_HEAD_DIM
D_DSA_Q = DSA_HEADS * DSA_KV_RANK
D_IDX_Q = IDX_HEADS * IDX_DIM
D_DSA_OUT = DSA_HEADS * DSA_HEAD_DIM

LANES = 128
SUBLANES = 8
ATT_TILE = 256
SB_Q_TILE = 2 * ATT_TILE
COUNT_CHAINS = 4
TOKEN_TILE = 512
FF_TILE = 1024
MOD_TILE = 1024
VMEM_LIMIT = 56 * 1024 * 1024

NEG_BIG = -1e30
INT_MIN = -2 ** 31
HALF_BIAS = 2 ** 15
LOG2E = 1.4426950408889634

C_QA, C_KA, C_VA = 0, D_SB, 2 * D_SB
C_DQ = 3 * D_SB
C_IQ = C_DQ + D_DSA_Q
C_KV = C_IQ + D_IDX_Q
C_IK = C_KV + DSA_KV_RANK
C_IW = C_IK + LANES
D_CAT = C_IW + LANES


def _nt(a, b):
    return lax.dot_general(a, b, (((1,), (1,)), ((), ())), preferred_element_type=F32)


def _dot(a, b):
    return jnp.dot(a, b, preferred_element_type=F32)


def _rms(v, axis=-1):
    return v * lax.rsqrt(jnp.mean(v * v, axis=axis, keepdims=True) + EPS)


def _mod_body(c_ref, w_ref, b_ref, o_ref):
    c = c_ref[...]
    ca = c / (1.0 + jnp.exp(-c))
    c_hi = ca.astype(BF16)
    c_lo = (ca - c_hi.astype(F32)).astype(BF16)
    w = w_ref[0]
    w_hi = w.astype(BF16)
    w_lo = (w - w_hi.astype(F32)).astype(BF16)
    o_ref[0] = _dot(c_hi, w_hi) + _dot(c_lo, w_hi) + _dot(c_hi, w_lo) + b_ref[0]


def _mod_call(c, w_mod, b_mod):
    depth, d, n = w_mod.shape
    bsz = c.shape[0]
    return pl.pallas_call(
        _mod_body,
        grid=(depth, n // MOD_TILE),
        in_specs=[
            pl.BlockSpec((bsz, d), lambda l, j: (0, 0)),
            pl.BlockSpec((1, d, MOD_TILE), lambda l, j: (l, 0, j)),
            pl.BlockSpec((1, 1, MOD_TILE), lambda l, j: (l, 0, j)),
        ],
        out_specs=pl.BlockSpec((1, bsz, MOD_TILE), lambda l, j: (l, 0, j)),
        out_shape=jax.ShapeDtypeStruct((depth, bsz, n), F32),
        compiler_params=pltpu.CompilerParams(vmem_limit_bytes=VMEM_LIMIT),
        name="adaln_mod",
    )(c, w_mod, b_mod.reshape(depth, 1, n))


def _inproj_body(x_ref, mod_ref, g_ref, w_ref, kvg_ref,
                 qa_ref, ka_ref, va_ref, dq_ref, kv_ref, kvt_ref, iq_ref, ik_ref, iwt_ref):
    x = x_ref[0]
    sh1 = mod_ref[0, 0:1, :]
    sc1 = mod_ref[0, 1:2, :]
    h = (_rms(x) * g_ref[...]) * (1.0 + sc1) + sh1
    h = h.astype(BF16)

    def proj(c0, width):
        return _dot(h, w_ref[:, c0:c0 + width])

    qa_ref[0] = (proj(C_QA, D_SB) * (SB_HEAD_DIM ** -0.5 * LOG2E)).astype(BF16)
    ka_ref[0] = proj(C_KA, D_SB).astype(BF16)
    va_ref[0] = proj(C_VA, D_SB).astype(BF16)
    dq = proj(C_DQ, D_DSA_Q) * (DSA_KV_RANK ** -0.5 * LOG2E)
    for hd in range(DSA_HEADS):
        dq_ref[0, hd] = dq[:, hd * DSA_KV_RANK:(hd + 1) * DSA_KV_RANK].astype(BF16)
    iq = proj(C_IQ, D_IDX_Q).astype(BF16)
    for hp in range(IDX_HEADS // 2):
        iq_ref[0, hp] = iq[:, hp * LANES:(hp + 1) * LANES]
    rest = proj(C_KV, D_CAT - C_KV)
    kv = _rms(rest[:, 0:DSA_KV_RANK]) * kvg_ref[...]
    kv_ref[0] = kv.astype(BF16)
    kv_t = kv.T.astype(BF16)
    for cblk in range(kvt_ref.shape[1]):
        kvt_ref[0, cblk] = kv_t[:, cblk * ATT_TILE:(cblk + 1) * ATT_TILE]
    ik_ref[0] = rest[:, C_IK - C_KV:C_IK - C_KV + LANES].astype(BF16)
    iwt_ref[0] = rest[:, C_IW - C_KV:].T[0:IDX_HEADS, :]


def _inproj_call(x, modl, g, wcat, kvg):
    bsz, s, d = x.shape
    tm = min(TOKEN_TILE, s)
    nkb = tm // ATT_TILE
    out_shapes = (
        jax.ShapeDtypeStruct((bsz, s, D_SB), BF16),
        jax.ShapeDtypeStruct((bsz, s, D_SB), BF16),
        jax.ShapeDtypeStruct((bsz, s, D_SB), BF16),
        jax.ShapeDtypeStruct((bsz, DSA_HEADS, s, DSA_KV_RANK), BF16),
        jax.ShapeDtypeStruct((bsz, s, DSA_KV_RANK), BF16),
        jax.ShapeDtypeStruct((bsz, s // ATT_TILE, DSA_KV_RANK, ATT_TILE), BF16),
        jax.ShapeDtypeStruct((bsz, IDX_HEADS // 2, s, LANES), BF16),
        jax.ShapeDtypeStruct((bsz, s, LANES), BF16),
        jax.ShapeDtypeStruct((bsz, IDX_HEADS, s), F32),
    )
    out_specs = (
        pl.BlockSpec((1, tm, D_SB), lambda b, i: (b, i, 0)),
        pl.BlockSpec((1, tm, D_SB), lambda b, i: (b, i, 0)),
        pl.BlockSpec((1, tm, D_SB), lambda b, i: (b, i, 0)),
        pl.BlockSpec((1, DSA_HEADS, tm, DSA_KV_RANK), lambda b, i: (b, 0, i, 0)),
        pl.BlockSpec((1, tm, DSA_KV_RANK), lambda b, i: (b, i, 0)),
        pl.BlockSpec((1, nkb, DSA_KV_RANK, ATT_TILE), lambda b, i: (b, i, 0, 0)),
        pl.BlockSpec((1, IDX_HEADS // 2, tm, LANES), lambda b, i: (b, 0, i, 0)),
        pl.BlockSpec((1, tm, LANES), lambda b, i: (b, i, 0)),
        pl.BlockSpec((1, IDX_HEADS, tm), lambda b, i: (b, 0, i)),
    )
    return pl.pallas_call(
        _inproj_body,
        grid=(bsz, s // tm),
        in_specs=[
            pl.BlockSpec((1, tm, d), lambda b, i: (b, i, 0)),
            pl.BlockSpec((1, 6, d), lambda b, i: (b, 0, 0)),
            pl.BlockSpec((1, d), lambda b, i: (0, 0)),
            pl.BlockSpec((d, D_CAT), lambda b, i: (0, 0)),
            pl.BlockSpec((1, DSA_KV_RANK), lambda b, i: (0, 0)),
        ],
        out_specs=out_specs,
        out_shape=out_shapes,
        compiler_params=pltpu.CompilerParams(
            dimension_semantics=("parallel", "parallel"), vmem_limit_bytes=VMEM_LIMIT),
        name="inproj",
    )(x, modl, g.reshape(1, d), wcat, kvg.reshape(1, DSA_KV_RANK))


def _sb_body(q_ref, k_ref, v_ref, tri_ref, o_ref, acc_ref, *, tq, tk):
    i = pl.program_id(2)
    lane = lax.broadcasted_iota(I32, (1, LANES), 1)
    in_half = (lane < SB_HEAD_DIM, lane >= SB_HEAD_DIM)
    tri = tri_ref[...]
    acc_ref[...] = jnp.zeros(acc_ref.shape, F32)

    def tile(j, carries, r0, diag_off):
        start = pl.multiple_of(j * tk, tk)
        kb = k_ref[0, pl.ds(start, tk), :]
        vb = v_ref[0, pl.ds(start, tk), :]
        q2 = q_ref[0, r0:tq, :]
        if diag_off is not None:
            row = lax.broadcasted_iota(I32, (tq - r0, tk), 0) + r0
            col = lax.broadcasted_iota(I32, (tq - r0, tk), 1) + diag_off
            causal = col < row
        out = []
        for half in range(2):
            kh = jnp.where(in_half[half], kb, jnp.zeros_like(kb))
            z = _nt(q2, kh)
            neg_abs = lax.bitcast_convert_type(
                lax.bitcast_convert_type(z, I32) | jnp.int32(INT_MIN), F32)
            sp = jnp.log(1.0 + jnp.exp2(neg_abs)) * LOG2E
            log_beta = jnp.minimum(z, 0.0) - sp
            log_1m = log_beta - z
            if diag_off is not None:
                log_1m = jnp.where(causal, log_1m, 0.0)
            after = _dot(log_1m.astype(BF16), tri)
            carry = carries[half][r0:tq]
            w = jnp.exp2(log_beta + after + carry)
            if diag_off is not None:
                w = jnp.where(causal, w, 0.0)
            acc_ref[half, r0:tq, :] += _dot(w.astype(BF16), vb)
            carry = carry + jnp.sum(log_1m, axis=-1, keepdims=True)
            if r0 > 0:
                carry = jnp.concatenate([carries[half][0:r0], carry], axis=0)
            out.append(carry)
        return tuple(out)

    zero = jnp.zeros((tq, 1), F32)
    carries = (zero, zero)
    n_diag = tq // tk
    for d in range(n_diag - 1, -1, -1):
        carries = tile(i * n_diag + d, carries, d * tk, d * tk)
    def far_group(n, cs):
        for d in range(n_diag):
            cs = tile((i - n) * n_diag - 1 - d, cs, 0, None)
        return cs

    carries = lax.fori_loop(0, i, far_group, carries)
    o_ref[0] = jnp.where(in_half[0], acc_ref[0], acc_ref[1])


def _sb_call(qa, ka, va, tri):
    bsz, s, _ = qa.shape
    tq, tk = min(SB_Q_TILE, s), ATT_TILE
    return pl.pallas_call(
        functools.partial(_sb_body, tq=tq, tk=tk),
        grid=(bsz, D_SB // LANES, s // tq),
        in_specs=[
            pl.BlockSpec((1, tq, LANES), lambda b, hp, i: (b, i, hp)),
            pl.BlockSpec((1, s, LANES), lambda b, hp, i: (b, 0, hp)),
            pl.BlockSpec((1, s, LANES), lambda b, hp, i: (b, 0, hp)),
            pl.BlockSpec((tk, tk), lambda b, hp, i: (0, 0)),
        ],
        out_specs=pl.BlockSpec((1, tq, LANES), lambda b, hp, i: (b, i, hp)),
        out_shape=jax.ShapeDtypeStruct((bsz, s, D_SB), F32),
        scratch_shapes=[pltpu.VMEM((2, tq, LANES), F32)],
        compiler_params=pltpu.CompilerParams(
            dimension_semantics=("parallel", "parallel", "parallel"), vmem_limit_bytes=VMEM_LIMIT),
        name="stickbreak_attn",
    )(qa, ka, va, tri)


def _dsa_body(dq_ref, kv_ref, kvt_ref, iq_ref, ik_ref, iwt_ref, bias_near_ref, bias_lone_ref, wuvt_ref,
              o_ref, key_ref, hi_ref, lo_ref, madd_ref, acc_ref, obt_ref, *, t, topk, idx_bits):
    i = pl.program_id(1)
    nblk = i + 1
    lane = lax.broadcasted_iota(I32, (1, LANES), 1)
    lo_half = lane < IDX_DIM
    srow = lax.broadcasted_iota(I32, (t, t), 0)
    qcol = lax.broadcasted_iota(I32, (t, t), 1)

    iq_pairs = iq_ref[0].reshape((IDX_HEADS // 2) * t, LANES)

    def score_keys(j):
        start = pl.multiple_of(j * t, t)
        ikb = ik_ref[0, pl.ds(start, t), :]
        zero = jnp.zeros_like(ikb)
        sc_half = (_nt(jnp.where(lo_half, ikb, zero), iq_pairs),
                   _nt(jnp.where(lo_half, zero, ikb), iq_pairs))
        acc = jnp.zeros((t, t), F32)
        for hd in range(IDX_HEADS):
            sc = sc_half[hd % 2][:, (hd // 2) * t:(hd // 2 + 1) * t]
            acc = acc + jnp.maximum(sc, 0.0) * iwt_ref[0, hd:hd + 1, :]
        bits = lax.bitcast_convert_type(acc, I32)
        return jnp.where(bits < 0, jnp.int32(INT_MIN) - bits, bits), start

    def store_keys(keys, start):
        key_ref[pl.ds(start, t), :] = keys
        hi_ref[pl.ds(start, t), :] = (keys >> 16).astype(I16)
        lo_ref[pl.ds(start, t), :] = ((keys & 0xFFFF) - HALF_BIAS).astype(I16)

    def store_far(j, carry):
        store_keys(*score_keys(j))
        return carry

    lax.fori_loop(0, i, store_far, 0)
    keys, start = score_keys(i)
    admissible = (srow >> 6) <= (qcol >> 6)
    store_keys(jnp.where(admissible, keys, jnp.int32(INT_MIN)), start)

    group = SUBLANES * COUNT_CHAINS
    group16 = 2 * group

    def col_count(pred_fn):
        def body(j, c):
            kb = key_ref[pl.ds(pl.multiple_of(j * t, t), t), :]
            ind = pred_fn(kb, j).astype(I32)
            return c + jnp.sum(ind.reshape(t // group, group, t), axis=0)
        c = lax.fori_loop(0, nblk, body, jnp.zeros((group, t), I32))
        return jnp.sum(c, axis=0, keepdims=True)

    def col_count16(ref, cand):
        def body(j, c):
            blk = ref[pl.ds(pl.multiple_of(j * t, t), t), :]
            ind = jnp.where(blk >= cand, jnp.int16(1), jnp.int16(0))
            for r in range(t // group16):
                c = c + ind[r * group16:(r + 1) * group16, :]
            return c
        c = lax.fori_loop(0, nblk, body, jnp.zeros((group16, t), I16))
        return jnp.sum(c.astype(I32), axis=0, keepdims=True)

    def half_search(ref, want):
        def bit_step(n, thr):
            cand = thr + lax.shift_left(jnp.int32(1), 15 - n)
            cnt = col_count16(ref, cand.astype(I16))
            return jnp.where(cnt >= want, cand, thr)
        return lax.fori_loop(0, 16, bit_step, jnp.full((1, t), -HALF_BIAS, I32))

    thr_hi = half_search(hi_ref, topk)
    above = col_count16(hi_ref, (thr_hi + 1).astype(I16))
    above = jnp.where(thr_hi == HALF_BIAS - 1, 0, above)
    thr_hi16 = thr_hi.astype(I16)

    def keep_bucket(j, carry):
        rows = pl.ds(pl.multiple_of(j * t, t), t)
        lo_ref[rows, :] = jnp.where(hi_ref[rows, :] == thr_hi16, lo_ref[rows, :], jnp.int16(-HALF_BIAS))
        return carry

    lax.fori_loop(0, nblk, keep_bucket, 0)
    thr_lo = half_search(lo_ref, topk - above)
    thr = (thr_hi << 16) | (thr_lo + HALF_BIAS)

    cnt_gt = col_count(lambda kb, j: kb > thr)
    cnt_eq = col_count(lambda kb, j: kb == thr)
    need = topk - cnt_gt
    real_thr = thr > INT_MIN
    surplus = jnp.logical_and(cnt_eq > need, real_thr)
    any_surplus = jnp.max(surplus.astype(I32)) > 0

    def index_cut():
        def index_bit(n, cut):
            cand = cut + lax.shift_left(jnp.int32(1), idx_bits - 1 - n)
            cnt = col_count(lambda kb, j: jnp.logical_and(kb == thr, srow + j * t < cand))
            return jnp.where(cnt <= need, cand, cut)
        return lax.fori_loop(0, idx_bits, index_bit, jnp.zeros((1, t), I32))

    cut = lax.cond(any_surplus, index_cut, lambda: jnp.full((1, t), 1 << idx_bits, I32))
    cut = jnp.where(real_thr, cut, 0)

    def store_mask(j, carry):
        start = pl.multiple_of(j * t, t)
        kb = key_ref[pl.ds(start, t), :]
        keep = jnp.logical_or(kb > thr, jnp.logical_and(kb == thr, srow + j * t < cut))
        madd_ref[pl.ds(start, t), :] = jnp.where(keep, 0.0, NEG_BIG)
        return carry

    lax.fori_loop(0, nblk, store_mask, 0)

    width = DSA_HEADS * t
    q_all = dq_ref[0].reshape(width, DSA_KV_RANK)
    acc_ref[...] = jnp.zeros(acc_ref.shape, F32)

    def attend(j0, n, state, bias_of_head):
        m_old, l_old = state
        rows = pl.ds(pl.multiple_of(j0 * t, t), n * t)
        s_all = _nt(kv_ref[0, rows, :], q_all)
        madd = madd_ref[rows, :]
        s_heads = []
        for hd in range(DSA_HEADS):
            s = s_all[:, hd * t:(hd + 1) * t] + madd
            bias = bias_of_head(hd)
            if bias is not None:
                s = s + bias
            s_heads.append(s)
        s_all = jnp.concatenate(s_heads, axis=1)
        m_new = jnp.maximum(m_old, jnp.max(s_all, axis=0, keepdims=True))
        alpha = jnp.exp2(m_old - m_new)
        p = jnp.exp2(s_all - m_new)
        l_new = l_old * alpha + jnp.sum(p, axis=0, keepdims=True)
        kvt = jnp.concatenate([kvt_ref[0, j0 + d] for d in range(n)], axis=1)
        acc_ref[...] = acc_ref[...] * alpha + _dot(kvt, p.astype(BF16))
        return m_new, l_new

    state = (jnp.full((1, width), NEG_BIG, F32), jnp.zeros((1, width), F32))
    lone_bias = jnp.where(i == 0, 1, 0)
    state = lax.cond(i % 2 == 0,
                     lambda st: attend(0, 1, st, lambda hd: bias_lone_ref[lone_bias, hd]),
                     lambda st: st, state)
    first = 1 - i % 2
    n_pairs = (i + 1) // 2
    state = lax.fori_loop(0, jnp.maximum(n_pairs - 1, 0),
                          lambda k, st: attend(first + 2 * k, 2, st, lambda hd: None), state)
    _, l_fin = lax.cond(i >= 1,
                        lambda st: attend(i - 1, 2, st, lambda hd: bias_near_ref[hd]),
                        lambda st: st, state)

    o_t = (acc_ref[...] * (1.0 / l_fin)).astype(BF16)
    for hd in range(DSA_HEADS):
        obt_ref[hd * DSA_HEAD_DIM:(hd + 1) * DSA_HEAD_DIM, :] = _dot(
            wuvt_ref[hd], o_t[:, hd * t:(hd + 1) * t])
    o_ref[0] = obt_ref[...].T


def _dsa_call(dq, kv, kvt, iq, ik2, iwt, bias_near, bias_lone, wuvt):
    bsz, s, _ = kv.shape
    t = ATT_TILE
    nkb = s // t
    topk = min(DSA_MAX_TOPK, s // 4)
    body = functools.partial(_dsa_body, t=t, topk=topk, idx_bits=int(s).bit_length())
    return pl.pallas_call(
        body,
        grid=(bsz, nkb),
        in_specs=[
            pl.BlockSpec((1, DSA_HEADS, t, DSA_KV_RANK), lambda b, i: (b, 0, i, 0)),
            pl.BlockSpec((1, s, DSA_KV_RANK), lambda b, i: (b, 0, 0)),
            pl.BlockSpec((1, nkb, DSA_KV_RANK, t), lambda b, i: (b, 0, 0, 0)),
            pl.BlockSpec((1, IDX_HEADS // 2, t, LANES), lambda b, i: (b, 0, i, 0)),
            pl.BlockSpec((1, s, LANES), lambda b, i: (b, 0, 0)),
            pl.BlockSpec((1, IDX_HEADS, t), lambda b, i: (b, 0, i)),
            pl.BlockSpec((DSA_HEADS, 2 * t, t), lambda b, i: (0, 0, 0)),
            pl.BlockSpec((2, DSA_HEADS, t, t), lambda b, i: (0, 0, 0, 0)),
            pl.BlockSpec((DSA_HEADS, DSA_HEAD_DIM, DSA_KV_RANK), lambda b, i: (0, 0, 0)),
        ],
        out_specs=pl.BlockSpec((1, t, D_DSA_OUT), lambda b, i: (b, i, 0)),
        out_shape=jax.ShapeDtypeStruct((bsz, s, D_DSA_OUT), F32),
        scratch_shapes=[
            pltpu.VMEM((s, t), I32),
            pltpu.VMEM((s, t), I16),
            pltpu.VMEM((s, t), I16),
            pltpu.VMEM((s, t), F32),
            pltpu.VMEM((DSA_KV_RANK, DSA_HEADS * t), F32),
            pltpu.VMEM((D_DSA_OUT, t), F32),
        ],
        compiler_params=pltpu.CompilerParams(
            dimension_semantics=("parallel", "parallel"), vmem_limit_bytes=VMEM_LIMIT),
        name="sparse_attn",
    )(dq, kv, kvt, iq, ik2, iwt, bias_near, bias_lone, wuvt)


def _post_body(oa_ref, ob_ref, x_ref, mod_ref, ga_ref, gb_ref, wout_ref, gm_ref, wup_ref, wdn_ref,
               gf_ref, o_ref, *, final):
    ga1 = mod_ref[0, 2:3, :]
    sh2 = mod_ref[0, 3:4, :]
    sc2 = mod_ref[0, 4:5, :]
    ga2 = mod_ref[0, 5:6, :]
    na = (_rms(oa_ref[0]) * ga_ref[...]).astype(BF16)
    nb = (_rms(ob_ref[0]) * gb_ref[...]).astype(BF16)
    attn = _dot(na, wout_ref[0:D_SB, :]) + _dot(nb, wout_ref[D_SB:, :])
    x1 = x_ref[0] + ga1 * attn
    h = ((_rms(x1) * gm_ref[...]) * (1.0 + sc2) + sh2).astype(BF16)
    acc = jnp.zeros_like(x1)
    for cblk in range(D_FF // FF_TILE):
        u = _dot(h, wup_ref[:, cblk * FF_TILE:(cblk + 1) * FF_TILE])
        r = jnp.maximum(u, 0.0)
        acc = acc + _dot((r * r).astype(BF16), wdn_ref[cblk * FF_TILE:(cblk + 1) * FF_TILE, :])
    x2 = x1 + ga2 * acc
    if final:
        x2 = _rms(x2) * gf_ref[...]
    o_ref[0] = x2


def _post_call(o_a, o_b, x, modl, g_a, g_b, w_out, g_mlp, w_up, w_down, g_final, final):
    bsz, s, d = x.shape
    tm = min(TOKEN_TILE, s)
    const2 = lambda b, i: (0, 0)
    return pl.pallas_call(
        functools.partial(_post_body, final=final),
        grid=(bsz, s // tm),
        in_specs=[
            pl.BlockSpec((1, tm, D_SB), lambda b, i: (b, i, 0)),
            pl.BlockSpec((1, tm, D_DSA_OUT), lambda b, i: (b, i, 0)),
            pl.BlockSpec((1, tm, d), lambda b, i: (b, i, 0)),
            pl.BlockSpec((1, 6, d), lambda b, i: (b, 0, 0)),
            pl.BlockSpec((1, D_SB), const2),
            pl.BlockSpec((1, D_DSA_OUT), const2),
            pl.BlockSpec((d, d), const2),
            pl.BlockSpec((1, d), const2),
            pl.BlockSpec((d, D_FF), const2),
            pl.BlockSpec((D_FF, d), const2),
            pl.BlockSpec((1, d), const2),
        ],
        out_specs=pl.BlockSpec((1, tm, d), lambda b, i: (b, i, 0)),
        out_shape=jax.ShapeDtypeStruct((bsz, s, d), F32),
        compiler_params=pltpu.CompilerParams(
            dimension_semantics=("parallel", "parallel"), vmem_limit_bytes=VMEM_LIMIT),
        name="outproj_mlp",
    )(o_a, o_b, x, modl, g_a.reshape(1, -1), g_b.reshape(1, -1), w_out, g_mlp.reshape(1, d),
      w_up, w_down, g_final.reshape(1, d))


def _t5_bucket_table(rel):
    nb = N_BUCKETS // 2
    max_exact = nb // 2
    n = np.abs(rel)
    nf = np.maximum(n, max_exact).astype(np.float64)
    large = max_exact + (np.log(nf / max_exact) / math.log(MAX_DISTANCE / max_exact)
                         * (nb - max_exact)).astype(np.int64)
    large = np.minimum(large, nb - 1)
    return np.where(rel > 0, nb, 0) + np.where(n < max_exact, n, large)


def _toeplitz_key_query(vec, t):
    width = 2 * t
    ext = jnp.pad(vec, ((0, 0), (0, 1)))
    flat = jnp.tile(ext, (1, t))[:, :t * (width - 1)]
    by_query = flat.reshape(vec.shape[0], t, width - 1)[:, :, t - 1:]
    return jnp.swapaxes(by_query, 1, 2)


def _bias_tiles(rel_bias, t, s):
    far_bucket = _t5_bucket_table(np.arange(-(s - 1), -t + 1))
    assert t >= MAX_DISTANCE and np.all(far_bucket == far_bucket[0])
    offs = np.arange(-(t - 1), t)
    tab = rel_bias.astype(F32)
    far = tab[int(far_bucket[0])]
    vec_diag = (tab[_t5_bucket_table(offs)] - far).T * LOG2E
    vec_prev = (tab[_t5_bucket_table(offs - t)] - far).T * LOG2E
    prev, diag = _toeplitz_key_query(vec_prev, t), _toeplitz_key_query(vec_diag, t)
    return jnp.concatenate([prev, diag], axis=1), jnp.stack([jnp.zeros_like(diag), diag])


def _cat_weight(w_in_l):
    c_sb = 3 * D_SB
    c_dq = c_sb + D_DSA_Q
    c_kv = c_dq + DSA_KV_RANK
    c_iq = c_kv + D_IDX_Q
    c_ik = c_iq + IDX_DIM
    kv = w_in_l[:, c_dq:c_kv]
    iq = w_in_l[:, c_kv:c_iq]
    ik = w_in_l[:, c_iq:c_ik]
    iw = w_in_l[:, c_ik:]
    pad = jnp.zeros((w_in_l.shape[0], LANES - IDX_HEADS), w_in_l.dtype)
    return jnp.concatenate([w_in_l[:, :c_dq], iq, kv, ik, ik, iw, pad], axis=1).astype(BF16)


def kernel(x, c, w_mod, b_mod, g_attn, w_in, kv_norm_g, w_uv, g_out_a, g_out_b, w_out, g_mlp,
           w_up, w_down, rel_bias, g_final):
    bsz, s, d = x.shape
    assert d == D_MODEL and s % TOKEN_TILE == 0 and TOKEN_TILE % ATT_TILE == 0
    mod = _mod_call(c, w_mod, b_mod)
    bias_near, bias_lone = _bias_tiles(rel_bias, ATT_TILE, s)
    tri = jnp.asarray(np.tril(np.ones((ATT_TILE, ATT_TILE), np.float32), -1), BF16)
    for l in range(DEPTH):
        modl = mod[l].reshape(bsz, 6, d)
        qa, ka, va, dq, kv, kvt, iq, ik2, iwt = _inproj_call(
            x, modl, g_attn[l], _cat_weight(w_in[l]), kv_norm_g[l])
        o_a = _sb_call(qa, ka, va, tri)
        wuvt = jnp.transpose(w_uv[l], (0, 2, 1)).astype(BF16)
        o_b = _dsa_call(dq, kv, kvt, iq, ik2, iwt, bias_near, bias_lone, wuvt)
        x = _post_call(o_a, o_b, x, modl, g_out_a[l], g_out_b[l], w_out[l].astype(BF16), g_mlp[l],
                       w_up[l].astype(BF16), w_down[l].astype(BF16), g_final, l == DEPTH - 1)
    return x
```

```python
import functools
import math

import numpy as np
import jax
import jax.numpy as jnp
from jax import lax
from jax.experimental import pallas as pl
from jax.experimental.pallas import tpu as pltpu

F32 = jnp.float32
BF16 = jnp.bfloat16
I32 = jnp.int32
I16 = jnp.int16

D_MODEL = 1024
DEPTH = 2
CHUNK = 64
SB_HEADS = 8
SB_HEAD_DIM = 64
DSA_HEADS = 8
DSA_HEAD_DIM = 64
DSA_KV_RANK = 128
IDX_HEADS = 8
IDX_DIM = 64
DSA_MAX_TOPK = 256
D_FF = 4 * D_MODEL
N_BUCKETS = 32
MAX_DISTANCE = 128
EPS = 1e-6

D_SB = SB_HEADS * SB_HEAD_DIM
D_DSA_Q = DSA_HEADS * DSA_KV_RANK
D_IDX_Q = IDX_HEADS * IDX_DIM
D_DSA_OUT = DSA_HEADS * DSA_HEAD_DIM

LANES = 128
SUBLANES = 8
ATT_TILE = 256
SB_Q_TILE = 2 * ATT_TILE
COUNT_CHAINS = 4
TOKEN_TILE = 512
FF_TILE = 1024
MOD_TILE = 1024
VMEM_LIMIT = 56 * 1024 * 1024

NEG_BIG = -1e30
INT_MIN = -2 ** 31
HALF_BIAS = 2 ** 15
LOG2E = 1.4426950408889634

C_QA, C_KA, C_VA = 0, D_SB, 2 * D_SB
C_DQ = 3 * D_SB
C_IQ = C_DQ + D_DSA_Q
C_KV = C_IQ + D_IDX_Q
C_IK = C_KV + DSA_KV_RANK
C_IW = C_IK + LANES
D_CAT = C_IW + LANES


def _nt(a, b):
    return lax.dot_general(a, b, (((1,), (1,)), ((), ())), preferred_element_type=F32)


def _dot(a, b):
    return jnp.dot(a, b, preferred_element_type=F32)


def _rms(v, axis=-1):
    return v * lax.rsqrt(jnp.mean(v * v, axis=axis, keepdims=True) + EPS)


def _mod_body(c_ref, w_ref, b_ref, o_ref):
    c = c_ref[...]
    ca = c / (1.0 + jnp.exp(-c))
    c_hi = ca.astype(BF16)
    c_lo = (ca - c_hi.astype(F32)).astype(BF16)
    w = w_ref[0]
    w_hi = w.astype(BF16)
    w_lo = (w - w_hi.astype(F32)).astype(BF16)
    o_ref[0] = _dot(c_hi, w_hi) + _dot(c_lo, w_hi) + _dot(c_hi, w_lo) + b_ref[0]


def _mod_call(c, w_mod, b_mod):
    depth, d, n = w_mod.shape
    bsz = c.shape[0]
    return pl.pallas_call(
        _mod_body,
        grid=(depth, n // MOD_TILE),
        in_specs=[
            pl.BlockSpec((bsz, d), lambda l, j: (0, 0)),
            pl.BlockSpec((1, d, MOD_TILE), lambda l, j: (l, 0, j)),
            pl.BlockSpec((1, 1, MOD_TILE), lambda l, j: (l, 0, j)),
        ],
        out_specs=pl.BlockSpec((1, bsz, MOD_TILE), lambda l, j: (l, 0, j)),
        out_shape=jax.ShapeDtypeStruct((depth, bsz, n), F32),
        compiler_params=pltpu.CompilerParams(vmem_limit_bytes=VMEM_LIMIT),
        name="adaln_mod",
    )(c, w_mod, b_mod.reshape(depth, 1, n))


def _inproj_body(x_ref, mod_ref, g_ref, w_ref, kvg_ref,
                 qa_ref, ka_ref, va_ref, dq_ref, kv_ref, kvt_ref, iq_ref, ik_ref, iwt_ref):
    x = x_ref[0]
    sh1 = mod_ref[0, 0:1, :]
    sc1 = mod_ref[0, 1:2, :]
    h = (_rms(x) * g_ref[...]) * (1.0 + sc1) + sh1
    h = h.astype(BF16)

    def proj(c0, width):
        return _dot(h, w_ref[:, c0:c0 + width])

    qa_ref[0] = (proj(C_QA, D_SB) * (SB_HEAD_DIM ** -0.5 * LOG2E)).astype(BF16)
    ka_ref[0] = proj(C_KA, D_SB).astype(BF16)
    va_ref[0] = proj(C_VA, D_SB).astype(BF16)
    dq = proj(C_DQ, D_DSA_Q) * (DSA_KV_RANK ** -0.5 * LOG2E)
    for hd in range(DSA_HEADS):
        dq_ref[0, hd] = dq[:, hd * DSA_KV_RANK:(hd + 1) * DSA_KV_RANK].astype(BF16)
    iq = proj(C_IQ, D_IDX_Q).astype(BF16)
    for hp in range(IDX_HEADS // 2):
        iq_ref[0, hp] = iq[:, hp * LANES:(hp + 1) * LANES]
    rest = proj(C_KV, D_CAT - C_KV)
    kv = _rms(rest[:, 0:DSA_KV_RANK]) * kvg_ref[...]
    kv_ref[0] = kv.astype(BF16)
    kv_t = kv.T.astype(BF16)
    for cblk in range(kvt_ref.shape[1]):
        kvt_ref[0, cblk] = kv_t[:, cblk * ATT_TILE:(cblk + 1) * ATT_TILE]
    ik_ref[0] = rest[:, C_IK - C_KV:C_IK - C_KV + LANES].astype(BF16)
    iwt_ref[0] = rest[:, C_IW - C_KV:].T[0:IDX_HEADS, :]


def _inproj_call(x, modl, g, wcat, kvg):
    bsz, s, d = x.shape
    tm = min(TOKEN_TILE, s)
    nkb = tm // ATT_TILE
    out_shapes = (
        jax.ShapeDtypeStruct((bsz, s, D_SB), BF16),
        jax.ShapeDtypeStruct((bsz, s, D_SB), BF16),
        jax.ShapeDtypeStruct((bsz, s, D_SB), BF16),
        jax.ShapeDtypeStruct((bsz, DSA_HEADS, s, DSA_KV_RANK), BF16),
        jax.ShapeDtypeStruct((bsz, s, DSA_KV_RANK), BF16),
        jax.ShapeDtypeStruct((bsz, s // ATT_TILE, DSA_KV_RANK, ATT_TILE), BF16),
        jax.ShapeDtypeStruct((bsz, IDX_HEADS // 2, s, LANES), BF16),
        jax.ShapeDtypeStruct((bsz, s, LANES), BF16),
        jax.ShapeDtypeStruct((bsz, IDX_HEADS, s), F32),
    )
    out_specs = (
        pl.BlockSpec((1, tm, D_SB), lambda b, i: (b, i, 0)),
        pl.BlockSpec((1, tm, D_SB), lambda b, i: (b, i, 0)),
        pl.BlockSpec((1, tm, D_SB), lambda b, i: (b, i, 0)),
        pl.BlockSpec((1, DSA_HEADS, tm, DSA_KV_RANK), lambda b, i: (b, 0, i, 0)),
        pl.BlockSpec((1, tm, DSA_KV_RANK), lambda b, i: (b, i, 0)),
        pl.BlockSpec((1, nkb, DSA_KV_RANK, ATT_TILE), lambda b, i: (b, i, 0, 0)),
        pl.BlockSpec((1, IDX_HEADS // 2, tm, LANES), lambda b, i: (b, 0, i, 0)),
        pl.BlockSpec((1, tm, LANES), lambda b, i: (b, i, 0)),
        pl.BlockSpec((1, IDX_HEADS, tm), lambda b, i: (b, 0, i)),
    )
    return pl.pallas_call(
        _inproj_body,
        grid=(bsz, s // tm),
        in_specs=[
            pl.BlockSpec((1, tm, d), lambda b, i: (b, i, 0)),
            pl.BlockSpec((1, 6, d), lambda b, i: (b, 0, 0)),
            pl.BlockSpec((1, d), lambda b, i: (0, 0)),
            pl.BlockSpec((d, D_CAT), lambda b, i: (0, 0)),
            pl.BlockSpec((1, DSA_KV_RANK), lambda b, i: (0, 0)),
        ],
        out_specs=out_specs,
        out_shape=out_shapes,
        compiler_params=pltpu.CompilerParams(
            dimension_semantics=("parallel", "parallel"), vmem_limit_bytes=VMEM_LIMIT),
        name="inproj",
    )(x, modl, g.reshape(1, d), wcat, kvg.reshape(1, DSA_KV_RANK))


def _sb_body(q_ref, k_ref, v_ref, tri_ref, o_ref, acc_ref, *, tq, tk, nq):
    lane = lax.broadcasted_iota(I32, (1, LANES), 1)
    in_half = (lane < SB_HEAD_DIM, lane >= SB_HEAD_DIM)
    tri = tri_ref[...]
    acc_ref[...] = jnp.zeros(acc_ref.shape, F32)

    def tile(k, j, carries, r0, diag_off):
        kb = k_ref[0, j * tk:(j + 1) * tk, :]
        vb = v_ref[0, j * tk:(j + 1) * tk, :]
        q_rows = slice(k * tq + r0, (k + 1) * tq)
        q2 = q_ref[0, q_rows, :]
        if diag_off is not None:
            row = lax.broadcasted_iota(I32, (tq - r0, tk), 0) + r0
            col = lax.broadcasted_iota(I32, (tq - r0, tk), 1) + diag_off
            causal = col < row
        out = []
        for half in range(2):
            kh = jnp.where(in_half[half], kb, jnp.zeros_like(kb))
            z = _nt(q2, kh)
            neg_abs = lax.bitcast_convert_type(
                lax.bitcast_convert_type(z, I32) | jnp.int32(INT_MIN), F32)
            sp = jnp.log(1.0 + jnp.exp2(neg_abs)) * LOG2E
            log_beta = jnp.minimum(z, 0.0) - sp
            log_1m = log_beta - z
            if diag_off is not None:
                log_1m = jnp.where(causal, log_1m, 0.0)
            after = _dot(log_1m.astype(BF16), tri)
            carry = carries[half][r0:tq]
            w = jnp.exp2(log_beta + after + carry)
            if diag_off is not None:
                w = jnp.where(causal, w, 0.0)
            acc_ref[half, q_rows, :] += _dot(w.astype(BF16), vb)
            carry = carry + jnp.sum(log_1m, axis=-1, keepdims=True)
            if r0 > 0:
                carry = jnp.concatenate([carries[half][0:r0], carry], axis=0)
            out.append(carry)
        return tuple(out)

    n_diag = tq // tk

    for k in range(nq):
        zero = jnp.zeros((tq, 1), F32)
        carries = (zero, zero)
        for d in range(n_diag - 1, -1, -1):
            carries = tile(k, k * n_diag + d, carries, d * tk, d * tk)
        for j in range(k * n_diag - 1, -1, -1):
            carries = tile(k, j, carries, 0, None)
    o_ref[0] = jnp.where(in_half[0], acc_ref[0], acc_ref[1])


def _sb_call(qa, ka, va, tri):
    bsz, s, _ = qa.shape
    tq, tk = min(SB_Q_TILE, s), ATT_TILE
    return pl.pallas_call(
        functools.partial(_sb_body, tq=tq, tk=tk, nq=s // tq),
        grid=(bsz, D_SB // LANES),
        in_specs=[
            pl.BlockSpec((1, s, LANES), lambda b, hp: (b, 0, hp)),
            pl.BlockSpec((1, s, LANES), lambda b, hp: (b, 0, hp)),
            pl.BlockSpec((1, s, LANES), lambda b, hp: (b, 0, hp)),
            pl.BlockSpec((tk, tk), lambda b, hp: (0, 0)),
        ],
        out_specs=pl.BlockSpec((1, s, LANES), lambda b, hp: (b, 0, hp)),
        out_shape=jax.ShapeDtypeStruct((bsz, s, D_SB), F32),
        scratch_shapes=[pltpu.VMEM((2, s, LANES), F32)],
        compiler_params=pltpu.CompilerParams(
            dimension_semantics=("parallel", "parallel"), vmem_limit_bytes=VMEM_LIMIT),
        name="stickbreak_attn",
    )(qa, ka, va, tri)


def _dsa_body(dq_ref, kv_ref, kvt_ref, iq_ref, ik_ref, iwt_ref, bias_near_ref, bias_lone_ref, wuvt_ref,
              o_ref, key_ref, hi_ref, lo_ref, madd_ref, acc_ref, obt_ref, *, t, topk, idx_bits):
    i = pl.program_id(1)
    nblk = i + 1
    lane = lax.broadcasted_iota(I32, (1, LANES), 1)
    lo_half = lane < IDX_DIM
    srow = lax.broadcasted_iota(I32, (t, t), 0)
    qcol = lax.broadcasted_iota(I32, (t, t), 1)

    iq_pairs = iq_ref[0].reshape((IDX_HEADS // 2) * t, LANES)

    def score_keys(j):
        start = pl.multiple_of(j * t, t)
        ikb = ik_ref[0, pl.ds(start, t), :]
        zero = jnp.zeros_like(ikb)
        sc_half = (_nt(jnp.where(lo_half, ikb, zero), iq_pairs),
                   _nt(jnp.where(lo_half, zero, ikb), iq_pairs))
        acc = jnp.zeros((t, t), F32)
        for hd in range(IDX_HEADS):
            sc = sc_half[hd % 2][:, (hd // 2) * t:(hd // 2 + 1) * t]
            acc = acc + jnp.maximum(sc, 0.0) * iwt_ref[0, hd:hd + 1, :]
        bits = lax.bitcast_convert_type(acc, I32)
        return jnp.where(bits < 0, jnp.int32(INT_MIN) - bits, bits), start

    def store_keys(keys, start):
        key_ref[pl.ds(start, t), :] = keys
        hi_ref[pl.ds(start, t), :] = (keys >> 16).astype(I16)
        lo_ref[pl.ds(start, t), :] = ((keys & 0xFFFF) - HALF_BIAS).astype(I16)

    def store_far(j, carry):
        store_keys(*score_keys(j))
        return carry

    lax.fori_loop(0, i, store_far, 0)
    keys, start = score_keys(i)
    admissible = (srow >> 6) <= (qcol >> 6)
    store_keys(jnp.where(admissible, keys, jnp.int32(INT_MIN)), start)

    group = SUBLANES * COUNT_CHAINS
    group16 = 2 * group

    def col_count(pred_fn):
        def body(j, c):
            kb = key_ref[pl.ds(pl.multiple_of(j * t, t), t), :]
            ind = pred_fn(kb, j).astype(I32)
            return c + jnp.sum(ind.reshape(t // group, group, t), axis=0)
        c = lax.fori_loop(0, nblk, body, jnp.zeros((group, t), I32))
        return jnp.sum(c, axis=0, keepdims=True)

    def col_count16(ref, cand):
        def body(j, c):
            blk = ref[pl.ds(pl.multiple_of(j * t, t), t), :]
            ind = jnp.where(blk >= cand, jnp.int16(1), jnp.int16(0))
            for r in range(t // group16):
                c = c + ind[r * group16:(r + 1) * group16, :]
            return c
        c = lax.fori_loop(0, nblk, body, jnp.zeros((group16, t), I16))
        return jnp.sum(c.astype(I32), axis=0, keepdims=True)

    def half_search(ref, want):
        def bit_step(n, thr):
            cand = thr + lax.shift_left(jnp.int32(1), 15 - n)
            cnt = col_count16(ref, cand.astype(I16))
            return jnp.where(cnt >= want, cand, thr)
        return lax.fori_loop(0, 16, bit_step, jnp.full((1, t), -HALF_BIAS, I32))

    thr_hi = half_search(hi_ref, topk)
    above = col_count16(hi_ref, (thr_hi + 1).astype(I16))
    above = jnp.where(thr_hi == HALF_BIAS - 1, 0, above)
    thr_hi16 = thr_hi.astype(I16)

    def keep_bucket(j, carry):
        rows = pl.ds(pl.multiple_of(j * t, t), t)
        lo_ref[rows, :] = jnp.where(hi_ref[rows, :] == thr_hi16, lo_ref[rows, :], jnp.int16(-HALF_BIAS))
        return carry

    lax.fori_loop(0, nblk, keep_bucket, 0)
    thr_lo = half_search(lo_ref, topk - above)
    thr = (thr_hi << 16) | (thr_lo + HALF_BIAS)

    cnt_gt = col_count(lambda kb, j: kb > thr)
    cnt_eq = col_count(lambda kb, j: kb == thr)
    need = topk - cnt_gt
    real_thr = thr > INT_MIN
    surplus = jnp.logical_and(cnt_eq > need, real_thr)
    any_surplus = jnp.max(surplus.astype(I32)) > 0

    def index_cut():
        def index_bit(n, cut):
            cand = cut + lax.shift_left(jnp.int32(1), idx_bits - 1 - n)
            cnt = col_count(lambda kb, j: jnp.logical_and(kb == thr, srow + j * t < cand))
            return jnp.where(cnt <= need, cand, cut)
        return lax.fori_loop(0, idx_bits, index_bit, jnp.zeros((1, t), I32))

    cut = lax.cond(any_surplus, index_cut, lambda: jnp.full((1, t), 1 << idx_bits, I32))
    cut = jnp.where(real_thr, cut, 0)

    def store_mask(j, carry):
        start = pl.multiple_of(j * t, t)
        kb = key_ref[pl.ds(start, t), :]
        keep = jnp.logical_or(kb > thr, jnp.logical_and(kb == thr, srow + j * t < cut))
        madd_ref[pl.ds(start, t), :] = jnp.where(keep, 0.0, NEG_BIG)
        return carry

    lax.fori_loop(0, nblk, store_mask, 0)

    width = DSA_HEADS * t
    q_all = dq_ref[0].reshape(width, DSA_KV_RANK)
    acc_ref[...] = jnp.zeros(acc_ref.shape, F32)

    def attend(j0, n, state, bias_of_head):
        m_old, l_old = state
        rows = pl.ds(pl.multiple_of(j0 * t, t), n * t)
        s_all = _nt(kv_ref[0, rows, :], q_all)
        madd = madd_ref[rows, :]
        s_heads = []
        for hd in range(DSA_HEADS):
            s = s_all[:, hd * t:(hd + 1) * t] + madd
            bias = bias_of_head(hd)
            if bias is not None:
                s = s + bias
            s_heads.append(s)
        s_all = jnp.concatenate(s_heads, axis=1)
        m_new = jnp.maximum(m_old, jnp.max(s_all, axis=0, keepdims=True))
        alpha = jnp.exp2(m_old - m_new)
        p = jnp.exp2(s_all - m_new)
        l_new = l_old * alpha + jnp.sum(p, axis=0, keepdims=True)
        kvt = jnp.concatenate([kvt_ref[0, j0 + d] for d in range(n)], axis=1)
        acc_ref[...] = acc_ref[...] * alpha + _dot(kvt, p.astype(BF16))
        return m_new, l_new

    state = (jnp.full((1, width), NEG_BIG, F32), jnp.zeros((1, width), F32))
    lone_bias = jnp.where(i == 0, 1, 0)
    state = lax.cond(i % 2 == 0,
                     lambda st: attend(0, 1, st, lambda hd: bias_lone_ref[lone_bias, hd]),
                     lambda st: st, state)
    first = 1 - i % 2
    n_pairs = (i + 1) // 2
    state = lax.fori_loop(0, jnp.maximum(n_pairs - 1, 0),
                          lambda k, st: attend(first + 2 * k, 2, st, lambda hd: None), state)
    _, l_fin = lax.cond(i >= 1,
                        lambda st: attend(i - 1, 2, st, lambda hd: bias_near_ref[hd]),
                        lambda st: st, state)

    o_t = (acc_ref[...] * (1.0 / l_fin)).astype(BF16)
    for hd in range(DSA_HEADS):
        obt_ref[hd * DSA_HEAD_DIM:(hd + 1) * DSA_HEAD_DIM, :] = _dot(
            wuvt_ref[hd], o_t[:, hd * t:(hd + 1) * t])
    o_ref[0] = obt_ref[...].T


def _dsa_call(dq, kv, kvt, iq, ik2, iwt, bias_near, bias_lone, wuvt):
    bsz, s, _ = kv.shape
    t = ATT_TILE
    nkb = s // t
    topk = min(DSA_MAX_TOPK, s // 4)
    body = functools.partial(_dsa_body, t=t, topk=topk, idx_bits=int(s).bit_length())
    return pl.pallas_call(
        body,
        grid=(bsz, nkb),
        in_specs=[
            pl.BlockSpec((1, DSA_HEADS, t, DSA_KV_RANK), lambda b, i: (b, 0, i, 0)),
            pl.BlockSpec((1, s, DSA_KV_RANK), lambda b, i: (b, 0, 0)),
            pl.BlockSpec((1, nkb, DSA_KV_RANK, t), lambda b, i: (b, 0, 0, 0)),
            pl.BlockSpec((1, IDX_HEADS // 2, t, LANES), lambda b, i: (b, 0, i, 0)),
            pl.BlockSpec((1, s, LANES), lambda b, i: (b, 0, 0)),
            pl.BlockSpec((1, IDX_HEADS, t), lambda b, i: (b, 0, i)),
            pl.BlockSpec((DSA_HEADS, 2 * t, t), lambda b, i: (0, 0, 0)),
            pl.BlockSpec((2, DSA_HEADS, t, t), lambda b, i: (0, 0, 0, 0)),
            pl.BlockSpec((DSA_HEADS, DSA_HEAD_DIM, DSA_KV_RANK), lambda b, i: (0, 0, 0)),
        ],
        out_specs=pl.BlockSpec((1, t, D_DSA_OUT), lambda b, i: (b, i, 0)),
        out_shape=jax.ShapeDtypeStruct((bsz, s, D_DSA_OUT), F32),
        scratch_shapes=[
            pltpu.VMEM((s, t), I32),
            pltpu.VMEM((s, t), I16),
            pltpu.VMEM((s, t), I16),
            pltpu.VMEM((s, t), F32),
            pltpu.VMEM((DSA_KV_RANK, DSA_HEADS * t), F32),
            pltpu.VMEM((D_DSA_OUT, t), F32),
        ],
        compiler_params=pltpu.CompilerParams(
            dimension_semantics=("parallel", "parallel"), vmem_limit_bytes=VMEM_LIMIT),
        name="sparse_attn",
    )(dq, kv, kvt, iq, ik2, iwt, bias_near, bias_lone, wuvt)


def _post_body(oa_ref, ob_ref, x_ref, mod_ref, ga_ref, gb_ref, wout_ref, gm_ref, wup_ref, wdn_ref,
               gf_ref, o_ref, *, final):
    ga1 = mod_ref[0, 2:3, :]
    sh2 = mod_ref[0, 3:4, :]
    sc2 = mod_ref[0, 4:5, :]
    ga2 = mod_ref[0, 5:6, :]
    na = (_rms(oa_ref[0]) * ga_ref[...]).astype(BF16)
    nb = (_rms(ob_ref[0]) * gb_ref[...]).astype(BF16)
    attn = _dot(na, wout_ref[0:D_SB, :]) + _dot(nb, wout_ref[D_SB:, :])
    x1 = x_ref[0] + ga1 * attn
    h = ((_rms(x1) * gm_ref[...]) * (1.0 + sc2) + sh2).astype(BF16)
    acc = jnp.zeros_like(x1)
    for cblk in range(D_FF // FF_TILE):
        u = _dot(h, wup_ref[:, cblk * FF_TILE:(cblk + 1) * FF_TILE])
        r = jnp.maximum(u, 0.0)
        acc = acc + _dot((r * r).astype(BF16), wdn_ref[cblk * FF_TILE:(cblk + 1) * FF_TILE, :])
    x2 = x1 + ga2 * acc
    if final:
        x2 = _rms(x2) * gf_ref[...]
    o_ref[0] = x2


def _post_call(o_a, o_b, x, modl, g_a, g_b, w_out, g_mlp, w_up, w_down, g_final, final):
    bsz, s, d = x.shape
    tm = min(TOKEN_TILE, s)
    const2 = lambda b, i: (0, 0)
    return pl.pallas_call(
        functools.partial(_post_body, final=final),
        grid=(bsz, s // tm),
        in_specs=[
            pl.BlockSpec((1, tm, D_SB), lambda b, i: (b, i, 0)),
            pl.BlockSpec((1, tm, D_DSA_OUT), lambda b, i: (b, i, 0)),
            pl.BlockSpec((1, tm, d), lambda b, i: (b, i, 0)),
            pl.BlockSpec((1, 6, d), lambda b, i: (b, 0, 0)),
            pl.BlockSpec((1, D_SB), const2),
            pl.BlockSpec((1, D_DSA_OUT), const2),
            pl.BlockSpec((d, d), const2),
            pl.BlockSpec((1, d), const2),
            pl.BlockSpec((d, D_FF), const2),
            pl.BlockSpec((D_FF, d), const2),
            pl.BlockSpec((1, d), const2),
        ],
        out_specs=pl.BlockSpec((1, tm, d), lambda b, i: (b, i, 0)),
        out_shape=jax.ShapeDtypeStruct((bsz, s, d), F32),
        compiler_params=pltpu.CompilerParams(
            dimension_semantics=("parallel", "parallel"), vmem_limit_bytes=VMEM_LIMIT),
        name="outproj_mlp",
    )(o_a, o_b, x, modl, g_a.reshape(1, -1), g_b.reshape(1, -1), w_out, g_mlp.reshape(1, d),
      w_up, w_down, g_final.reshape(1, d))


def _t5_bucket_table(rel):
    nb = N_BUCKETS // 2
    max_exact = nb // 2
    n = np.abs(rel)
    nf = np.maximum(n, max_exact).astype(np.float64)
    large = max_exact + (np.log(nf / max_exact) / math.log(MAX_DISTANCE / max_exact)
                         * (nb - max_exact)).astype(np.int64)
    large = np.minimum(large, nb - 1)
    return np.where(rel > 0, nb, 0) + np.where(n < max_exact, n, large)


def _toeplitz_key_query(vec, t):
    width = 2 * t
    ext = jnp.pad(vec, ((0, 0), (0, 1)))
    flat = jnp.tile(ext, (1, t))[:, :t * (width - 1)]
    by_query = flat.reshape(vec.shape[0], t, width - 1)[:, :, t - 1:]
    return jnp.swapaxes(by_query, 1, 2)


def _bias_tiles(rel_bias, t, s):
    far_bucket = _t5_bucket_table(np.arange(-(s - 1), -t + 1))
    assert t >= MAX_DISTANCE and np.all(far_bucket == far_bucket[0])
    offs = np.arange(-(t - 1), t)
    tab = rel_bias.astype(F32)
    far = tab[int(far_bucket[0])]
    vec_diag = (tab[_t5_bucket_table(offs)] - far).T * LOG2E
    vec_prev = (tab[_t5_bucket_table(offs - t)] - far).T * LOG2E
    prev, diag = _toeplitz_key_query(vec_prev, t), _toeplitz_key_query(vec_diag, t)
    return jnp.concatenate([prev, diag], axis=1), jnp.stack([jnp.zeros_like(diag), diag])


def _cat_weight(w_in_l):
    c_sb = 3 * D_SB
    c_dq = c_sb + D_DSA_Q
    c_kv = c_dq + DSA_KV_RANK
    c_iq = c_kv + D_IDX_Q
    c_ik = c_iq + IDX_DIM
    kv = w_in_l[:, c_dq:c_kv]
    iq = w_in_l[:, c_kv:c_iq]
    ik = w_in_l[:, c_iq:c_ik]
    iw = w_in_l[:, c_ik:]
    pad = jnp.zeros((w_in_l.shape[0], LANES - IDX_HEADS), w_in_l.dtype)
    return jnp.concatenate([w_in_l[:, :c_dq], iq, kv, ik, ik, iw, pad], axis=1).astype(BF16)


def kernel(x, c, w_mod, b_mod, g_attn, w_in, kv_norm_g, w_uv, g_out_a, g_out_b, w_out, g_mlp,
           w_up, w_down, rel_bias, g_final):
    bsz, s, d = x.shape
    assert d == D_MODEL and s % TOKEN_TILE == 0 and TOKEN_TILE % ATT_TILE == 0
    mod = _mod_call(c, w_mod, b_mod)
    bias_near, bias_lone = _bias_tiles(rel_bias, ATT_TILE, s)
    tri = jnp.asarray(np.tril(np.ones((ATT_TILE, ATT_TILE), np.float32), -1), BF16)
    for l in range(DEPTH):
        modl = mod[l].reshape(bsz, 6, d)
        qa, ka, va, dq, kv, kvt, iq, ik2, iwt = _inproj_call(
            x, modl, g_attn[l], _cat_weight(w_in[l]), kv_norm_g[l])
        o_a = _sb_call(qa, ka, va, tri)
        wuvt = jnp.transpose(w_uv[l], (0, 2, 1)).astype(BF16)
        o_b = _dsa_call(dq, kv, kvt, iq, ik2, iwt, bias_near, bias_lone, wuvt)
        x = _post_call(o_a, o_b, x, modl, g_out_a[l], g_out_b[l], w_out[l].astype(BF16), g_mlp[l],
                       w_up[l].astype(BF16), w_down[l].astype(BF16), g_final, l == DEPTH - 1)
    return x
```

```python
import functools
import math

import numpy as np
import jax
import jax.numpy as jnp
from jax import lax
from jax.experimental import pallas as pl
from jax.experimental.pallas import tpu as pltpu

F32 = jnp.float32
BF16 = jnp.bfloat16
I32 = jnp.int32
I16 = jnp.int16

D_MODEL = 1024
DEPTH = 2
CHUNK = 64
SB_HEADS = 8
SB_HEAD_DIM = 64
DSA_HEADS = 8
DSA_HEAD_DIM = 64
DSA_KV_RANK = 128
IDX_HEADS = 8
IDX_DIM = 64
DSA_MAX_TOPK = 256
D_FF = 4 * D_MODEL
N_BUCKETS = 32
MAX_DISTANCE = 128
EPS = 1e-6

D_SB = SB_HEADS * SB_HEAD_DIM
D_DSA_Q = DSA_HEADS * DSA_KV_RANK
D_IDX_Q = IDX_HEADS * IDX_DIM
D_DSA_OUT = DSA_HEADS * DSA_HEAD_DIM

LANES = 128
SUBLANES = 8
ATT_TILE = 256
SB_Q_TILE = 2 * ATT_TILE
COUNT_CHAINS = 4
TOKEN_TILE = 512
FF_TILE = 1024
MOD_TILE = 1024
VMEM_LIMIT = 56 * 1024 * 1024

NEG_BIG = -1e30
INT_MIN = -2 ** 31
HALF_BIAS = 2 ** 15
RANK_NEVER = 1e9
LOG2E = 1.4426950408889634

C_QA, C_KA, C_VA = 0, D_SB, 2 * D_SB
C_DQ = 3 * D_SB
C_IQ = C_DQ + D_DSA_Q
C_KV = C_IQ + D_IDX_Q
C_IK = C_KV + DSA_KV_RANK
C_IW = C_IK + LANES
D_CAT = C_IW + LANES


def _nt(a, b):
    return lax.dot_general(a, b, (((1,), (1,)), ((), ())), preferred_element_type=F32)


def _dot(a, b):
    return jnp.dot(a, b, preferred_element_type=F32)


def _rms(v, axis=-1):
    return v * lax.rsqrt(jnp.mean(v * v, axis=axis, keepdims=True) + EPS)


def _mod_body(c_ref, w_ref, b_ref, o_ref):
    c = c_ref[...]
    ca = c / (1.0 + jnp.exp(-c))
    c_hi = ca.astype(BF16)
    c_lo = (ca - c_hi.astype(F32)).astype(BF16)
    w = w_ref[0]
    w_hi = w.astype(BF16)
    w_lo = (w - w_hi.astype(F32)).astype(BF16)
    o_ref[0] = _dot(c_hi, w_hi) + _dot(c_lo, w_hi) + _dot(c_hi, w_lo) + b_ref[0]


def _mod_call(c, w_mod, b_mod):
    depth, d, n = w_mod.shape
    bsz = c.shape[0]
    return pl.pallas_call(
        _mod_body,
        grid=(depth, n // MOD_TILE),
        in_specs=[
            pl.BlockSpec((bsz, d), lambda l, j: (0, 0)),
            pl.BlockSpec((1, d, MOD_TILE), lambda l, j: (l, 0, j)),
            pl.BlockSpec((1, 1, MOD_TILE), lambda l, j: (l, 0, j)),
        ],
        out_specs=pl.BlockSpec((1, bsz, MOD_TILE), lambda l, j: (l, 0, j)),
        out_shape=jax.ShapeDtypeStruct((depth, bsz, n), F32),
        compiler_params=pltpu.CompilerParams(vmem_limit_bytes=VMEM_LIMIT),
        name="adaln_mod",
    )(c, w_mod, b_mod.reshape(depth, 1, n))


def _inproj_body(x_ref, mod_ref, g_ref, w_ref, kvg_ref,
                 qa_ref, ka_ref, va_ref, dq_ref, kv_ref, kvt_ref, iq_ref, ik_ref, iwt_ref):
    x = x_ref[0]
    sh1 = mod_ref[0, 0:1, :]
    sc1 = mod_ref[0, 1:2, :]
    h = (_rms(x) * g_ref[...]) * (1.0 + sc1) + sh1
    h = h.astype(BF16)

    def proj(c0, width):
        return _dot(h, w_ref[:, c0:c0 + width])

    qa_ref[0] = (proj(C_QA, D_SB) * (SB_HEAD_DIM ** -0.5 * LOG2E)).astype(BF16)
    ka_ref[0] = proj(C_KA, D_SB).astype(BF16)
    va_ref[0] = proj(C_VA, D_SB).astype(BF16)
    dq = proj(C_DQ, D_DSA_Q) * (DSA_KV_RANK ** -0.5 * LOG2E)
    for hd in range(DSA_HEADS):
        dq_ref[0, hd] = dq[:, hd * DSA_KV_RANK:(hd + 1) * DSA_KV_RANK].astype(BF16)
    iq = proj(C_IQ, D_IDX_Q).astype(BF16)
    for hp in range(IDX_HEADS // 2):
        iq_ref[0, hp] = iq[:, hp * LANES:(hp + 1) * LANES]
    rest = proj(C_KV, D_CAT - C_KV)
    kv = _rms(rest[:, 0:DSA_KV_RANK]) * kvg_ref[...]
    kv_ref[0] = kv.astype(BF16)
    kv_t = kv.T.astype(BF16)
    for cblk in range(kvt_ref.shape[1]):
        kvt_ref[0, cblk] = kv_t[:, cblk * ATT_TILE:(cblk + 1) * ATT_TILE]
    ik_ref[0] = rest[:, C_IK - C_KV:C_IK - C_KV + LANES].astype(BF16)
    iwt_ref[0] = rest[:, C_IW - C_KV:].T[0:IDX_HEADS, :]


def _inproj_call(x, modl, g, wcat, kvg):
    bsz, s, d = x.shape
    tm = min(TOKEN_TILE, s)
    nkb = tm // ATT_TILE
    out_shapes = (
        jax.ShapeDtypeStruct((bsz, s, D_SB), BF16),
        jax.ShapeDtypeStruct((bsz, s, D_SB), BF16),
        jax.ShapeDtypeStruct((bsz, s, D_SB), BF16),
        jax.ShapeDtypeStruct((bsz, DSA_HEADS, s, DSA_KV_RANK), BF16),
        jax.ShapeDtypeStruct((bsz, s, DSA_KV_RANK), BF16),
        jax.ShapeDtypeStruct((bsz, s // ATT_TILE, DSA_KV_RANK, ATT_TILE), BF16),
        jax.ShapeDtypeStruct((bsz, IDX_HEADS // 2, s, LANES), BF16),
        jax.ShapeDtypeStruct((bsz, s, LANES), BF16),
        jax.ShapeDtypeStruct((bsz, IDX_HEADS, s), F32),
    )
    out_specs = (
        pl.BlockSpec((1, tm, D_SB), lambda b, i: (b, i, 0)),
        pl.BlockSpec((1, tm, D_SB), lambda b, i: (b, i, 0)),
        pl.BlockSpec((1, tm, D_SB), lambda b, i: (b, i, 0)),
        pl.BlockSpec((1, DSA_HEADS, tm, DSA_KV_RANK), lambda b, i: (b, 0, i, 0)),
        pl.BlockSpec((1, tm, DSA_KV_RANK), lambda b, i: (b, i, 0)),
        pl.BlockSpec((1, nkb, DSA_KV_RANK, ATT_TILE), lambda b, i: (b, i, 0, 0)),
        pl.BlockSpec((1, IDX_HEADS // 2, tm, LANES), lambda b, i: (b, 0, i, 0)),
        pl.BlockSpec((1, tm, LANES), lambda b, i: (b, i, 0)),
        pl.BlockSpec((1, IDX_HEADS, tm), lambda b, i: (b, 0, i)),
    )
    return pl.pallas_call(
        _inproj_body,
        grid=(bsz, s // tm),
        in_specs=[
            pl.BlockSpec((1, tm, d), lambda b, i: (b, i, 0)),
            pl.BlockSpec((1, 6, d), lambda b, i: (b, 0, 0)),
            pl.BlockSpec((1, d), lambda b, i: (0, 0)),
            pl.BlockSpec((d, D_CAT), lambda b, i: (0, 0)),
            pl.BlockSpec((1, DSA_KV_RANK), lambda b, i: (0, 0)),
        ],
        out_specs=out_specs,
        out_shape=out_shapes,
        compiler_params=pltpu.CompilerParams(
            dimension_semantics=("parallel", "parallel"), vmem_limit_bytes=VMEM_LIMIT),
        name="inproj",
    )(x, modl, g.reshape(1, d), wcat, kvg.reshape(1, DSA_KV_RANK))


def _sb_body(q_ref, k_ref, v_ref, tri_ref, o_ref, acc_ref, *, tq, tk, nq):
    lane = lax.broadcasted_iota(I32, (1, LANES), 1)
    in_half = (lane < SB_HEAD_DIM, lane >= SB_HEAD_DIM)
    tri = tri_ref[...]
    acc_ref[...] = jnp.zeros(acc_ref.shape, F32)

    def tile(k, j, carries, r0, diag_off):
        kb = k_ref[0, j * tk:(j + 1) * tk, :]
        vb = v_ref[0, j * tk:(j + 1) * tk, :]
        q_rows = slice(k * tq + r0, (k + 1) * tq)
        q2 = q_ref[0, q_rows, :]
        if diag_off is not None:
            row = lax.broadcasted_iota(I32, (tq - r0, tk), 0) + r0
            col = lax.broadcasted_iota(I32, (tq - r0, tk), 1) + diag_off
            causal = col < row
        out = []
        for half in range(2):
            kh = jnp.where(in_half[half], kb, jnp.zeros_like(kb))
            z = _nt(q2, kh)
            neg_abs = lax.bitcast_convert_type(
                lax.bitcast_convert_type(z, I32) | jnp.int32(INT_MIN), F32)
            sp = jnp.log(1.0 + jnp.exp2(neg_abs)) * LOG2E
            log_beta = jnp.minimum(z, 0.0) - sp
            log_1m = log_beta - z
            if diag_off is not None:
                log_1m = jnp.where(causal, log_1m, 0.0)
            after = _dot(log_1m.astype(BF16), tri)
            carry = carries[half][r0:tq]
            w = jnp.exp2(log_beta + after + carry)
            if diag_off is not None:
                w = jnp.where(causal, w, 0.0)
            acc_ref[half, q_rows, :] += _dot(w.astype(BF16), vb)
            carry = carry + jnp.sum(log_1m, axis=-1, keepdims=True)
            if r0 > 0:
                carry = jnp.concatenate([carries[half][0:r0], carry], axis=0)
            out.append(carry)
        return tuple(out)

    n_diag = tq // tk

    for k in range(nq):
        zero = jnp.zeros((tq, 1), F32)
        carries = (zero, zero)
        for d in range(n_diag - 1, -1, -1):
            carries = tile(k, k * n_diag + d, carries, d * tk, d * tk)
        for j in range(k * n_diag - 1, -1, -1):
            carries = tile(k, j, carries, 0, None)
    o_ref[0] = jnp.where(in_half[0], acc_ref[0], acc_ref[1])


def _sb_call(qa, ka, va, tri):
    bsz, s, _ = qa.shape
    tq, tk = min(SB_Q_TILE, s), ATT_TILE
    return pl.pallas_call(
        functools.partial(_sb_body, tq=tq, tk=tk, nq=s // tq),
        grid=(bsz, D_SB // LANES),
        in_specs=[
            pl.BlockSpec((1, s, LANES), lambda b, hp: (b, 0, hp)),
            pl.BlockSpec((1, s, LANES), lambda b, hp: (b, 0, hp)),
            pl.BlockSpec((1, s, LANES), lambda b, hp: (b, 0, hp)),
            pl.BlockSpec((tk, tk), lambda b, hp: (0, 0)),
        ],
        out_specs=pl.BlockSpec((1, s, LANES), lambda b, hp: (b, 0, hp)),
        out_shape=jax.ShapeDtypeStruct((bsz, s, D_SB), F32),
        scratch_shapes=[pltpu.VMEM((2, s, LANES), F32)],
        compiler_params=pltpu.CompilerParams(
            dimension_semantics=("parallel", "parallel"), vmem_limit_bytes=VMEM_LIMIT),
        name="stickbreak_attn",
    )(qa, ka, va, tri)


def _dsa_body(dq_ref, kv_ref, kvt_ref, iq_ref, ik_ref, iwt_ref, bias_near_ref, bias_lone_ref, wuvt_ref,
              tri_ref, o_ref, key_ref, hi_ref, lo_ref, madd_ref, acc_ref, obt_ref, *, t, topk, nkb):
    i = pl.program_id(1)
    nblk = i + 1
    lane = lax.broadcasted_iota(I32, (1, LANES), 1)
    lo_half = lane < IDX_DIM
    srow = lax.broadcasted_iota(I32, (t, t), 0)
    qcol = lax.broadcasted_iota(I32, (t, t), 1)

    iq_pairs = iq_ref[0].reshape((IDX_HEADS // 2) * t, LANES)

    def score_keys(j):
        start = j * t
        ikb = ik_ref[0, pl.ds(start, t), :]
        zero = jnp.zeros_like(ikb)
        sc_half = (_nt(jnp.where(lo_half, ikb, zero), iq_pairs),
                   _nt(jnp.where(lo_half, zero, ikb), iq_pairs))
        acc = jnp.zeros((t, t), F32)
        for hd in range(IDX_HEADS):
            sc = sc_half[hd % 2][:, (hd // 2) * t:(hd // 2 + 1) * t]
            acc = acc + jnp.maximum(sc, 0.0) * iwt_ref[0, hd:hd + 1, :]
        bits = lax.bitcast_convert_type(acc, I32)
        return jnp.where(bits < 0, jnp.int32(INT_MIN) - bits, bits), start

    def store_keys(keys, start):
        key_ref[pl.ds(start, t), :] = keys
        hi_ref[pl.ds(start, t), :] = (keys >> 16).astype(I16)
        lo_ref[pl.ds(start, t), :] = ((keys & 0xFFFF) - HALF_BIAS).astype(I16)

    def score_tiles(diag):
        def run():
            for j in range(diag):
                store_keys(*score_keys(j))
            keys, start = score_keys(diag)
            admissible = (srow >> 6) <= (qcol >> 6)
            store_keys(jnp.where(admissible, keys, jnp.int32(INT_MIN)), start)
        return run

    lax.switch(i, [score_tiles(n) for n in range(nkb)])

    group16 = 2 * SUBLANES * COUNT_CHAINS

    def col_count16(ref, cand):
        def body(j, c):
            blk = ref[pl.ds(pl.multiple_of(j * t, t), t), :]
            ind = jnp.where(blk >= cand, jnp.int16(1), jnp.int16(0))
            for r in range(t // group16):
                c = c + ind[r * group16:(r + 1) * group16, :]
            return c
        c = lax.fori_loop(0, nblk, body, jnp.zeros((group16, t), I16))
        return jnp.sum(c.astype(I32), axis=0, keepdims=True)

    def half_search(ref, want):
        def bit_step(n, thr):
            cand = thr + lax.shift_left(jnp.int32(1), 15 - n)
            cnt = col_count16(ref, cand.astype(I16))
            return jnp.where(cnt >= want, cand, thr)
        return lax.fori_loop(0, 16, bit_step, jnp.full((1, t), -HALF_BIAS, I32))

    thr_hi = half_search(hi_ref, topk)
    above = col_count16(hi_ref, (thr_hi + 1).astype(I16))
    above = jnp.where(thr_hi == HALF_BIAS - 1, 0, above)
    thr_hi16 = thr_hi.astype(I16)

    def keep_bucket(j, carry):
        rows = pl.ds(pl.multiple_of(j * t, t), t)
        lo_ref[rows, :] = jnp.where(hi_ref[rows, :] == thr_hi16, lo_ref[rows, :], jnp.int16(-HALF_BIAS))
        return carry

    lax.fori_loop(0, nblk, keep_bucket, 0)
    thr_lo = half_search(lo_ref, topk - above)
    thr = (thr_hi << 16) | (thr_lo + HALF_BIAS)

    gt_lo = col_count16(lo_ref, (thr_lo + 1).astype(I16))
    gt_lo = jnp.where(thr_lo == HALF_BIAS - 1, 0, gt_lo)
    real_thr = thr > INT_MIN
    need = jnp.where(real_thr, topk - above - gt_lo, 0).astype(F32)
    tri = tri_ref[...]

    def store_masks(n_tiles):
        def run():
            seen = jnp.zeros((1, t), F32)
            for j in range(n_tiles):
                kb = key_ref[j * t:(j + 1) * t, :]
                is_eq = kb == thr
                eq = jnp.where(is_eq, 1.0, 0.0)
                rank = _dot(tri, eq.astype(BF16)) + seen
                order = jnp.where(is_eq, rank, jnp.where(kb > thr, -1.0, RANK_NEVER))
                madd_ref[j * t:(j + 1) * t, :] = jnp.where(order < need, 0.0, NEG_BIG)
                seen = seen + jnp.sum(eq, axis=0, keepdims=True)
        return run

    lax.switch(i, [store_masks(n + 1) for n in range(nkb)])

    width = DSA_HEADS * t
    q_all = dq_ref[0].reshape(width, DSA_KV_RANK)
    acc_ref[...] = jnp.zeros(acc_ref.shape, F32)

    def attend(j0, n, state, bias_of_head):
        m_old, l_old = state
        rows = pl.ds(pl.multiple_of(j0 * t, t), n * t)
        s_all = _nt(kv_ref[0, rows, :], q_all)
        madd = madd_ref[rows, :]
        s_heads = []
        for hd in range(DSA_HEADS):
            s = s_all[:, hd * t:(hd + 1) * t] + madd
            bias = bias_of_head(hd)
            if bias is not None:
                s = s + bias
            s_heads.append(s)
        s_all = jnp.concatenate(s_heads, axis=1)
        m_new = jnp.maximum(m_old, jnp.max(s_all, axis=0, keepdims=True))
        alpha = jnp.exp2(m_old - m_new)
        p = jnp.exp2(s_all - m_new)
        l_new = l_old * alpha + jnp.sum(p, axis=0, keepdims=True)
        kvt = jnp.concatenate([kvt_ref[0, j0 + d] for d in range(n)], axis=1)
        acc_ref[...] = acc_ref[...] * alpha + _dot(kvt, p.astype(BF16))
        return m_new, l_new

    state = (jnp.full((1, width), NEG_BIG, F32), jnp.zeros((1, width), F32))
    lone_bias = jnp.where(i == 0, 1, 0)
    state = lax.cond(i % 2 == 0,
                     lambda st: attend(0, 1, st, lambda hd: bias_lone_ref[lone_bias, hd]),
                     lambda st: st, state)
    first = 1 - i % 2
    n_pairs = (i + 1) // 2
    state = lax.fori_loop(0, jnp.maximum(n_pairs - 1, 0),
                          lambda k, st: attend(first + 2 * k, 2, st, lambda hd: None), state)
    _, l_fin = lax.cond(i >= 1,
                        lambda st: attend(i - 1, 2, st, lambda hd: bias_near_ref[hd]),
                        lambda st: st, state)

    o_t = (acc_ref[...] * (1.0 / l_fin)).astype(BF16)
    for hd in range(DSA_HEADS):
        obt_ref[hd * DSA_HEAD_DIM:(hd + 1) * DSA_HEAD_DIM, :] = _dot(
            wuvt_ref[hd], o_t[:, hd * t:(hd + 1) * t])
    o_ref[0] = obt_ref[...].T


def _dsa_call(dq, kv, kvt, iq, ik2, iwt, bias_near, bias_lone, wuvt, tri):
    bsz, s, _ = kv.shape
    t = ATT_TILE
    nkb = s // t
    topk = min(DSA_MAX_TOPK, s // 4)
    body = functools.partial(_dsa_body, t=t, topk=topk, nkb=nkb)
    return pl.pallas_call(
        body,
        grid=(bsz, nkb),
        in_specs=[
            pl.BlockSpec((1, DSA_HEADS, t, DSA_KV_RANK), lambda b, i: (b, 0, i, 0)),
            pl.BlockSpec((1, s, DSA_KV_RANK), lambda b, i: (b, 0, 0)),
            pl.BlockSpec((1, nkb, DSA_KV_RANK, t), lambda b, i: (b, 0, 0, 0)),
            pl.BlockSpec((1, IDX_HEADS // 2, t, LANES), lambda b, i: (b, 0, i, 0)),
            pl.BlockSpec((1, s, LANES), lambda b, i: (b, 0, 0)),
            pl.BlockSpec((1, IDX_HEADS, t), lambda b, i: (b, 0, i)),
            pl.BlockSpec((DSA_HEADS, 2 * t, t), lambda b, i: (0, 0, 0)),
            pl.BlockSpec((2, DSA_HEADS, t, t), lambda b, i: (0, 0, 0, 0)),
            pl.BlockSpec((DSA_HEADS, DSA_HEAD_DIM, DSA_KV_RANK), lambda b, i: (0, 0, 0)),
            pl.BlockSpec((t, t), lambda b, i: (0, 0)),
        ],
        out_specs=pl.BlockSpec((1, t, D_DSA_OUT), lambda b, i: (b, i, 0)),
        out_shape=jax.ShapeDtypeStruct((bsz, s, D_DSA_OUT), F32),
        scratch_shapes=[
            pltpu.VMEM((s, t), I32),
            pltpu.VMEM((s, t), I16),
            pltpu.VMEM((s, t), I16),
            pltpu.VMEM((s, t), F32),
            pltpu.VMEM((DSA_KV_RANK, DSA_HEADS * t), F32),
            pltpu.VMEM((D_DSA_OUT, t), F32),
        ],
        compiler_params=pltpu.CompilerParams(
            dimension_semantics=("parallel", "parallel"), vmem_limit_bytes=VMEM_LIMIT),
        name="sparse_attn",
    )(dq, kv, kvt, iq, ik2, iwt, bias_near, bias_lone, wuvt, tri)


def _post_body(oa_ref, ob_ref, x_ref, mod_ref, ga_ref, gb_ref, wout_ref, gm_ref, wup_ref, wdn_ref,
               gf_ref, o_ref, *, final):
    ga1 = mod_ref[0, 2:3, :]
    sh2 = mod_ref[0, 3:4, :]
    sc2 = mod_ref[0, 4:5, :]
    ga2 = mod_ref[0, 5:6, :]
    na = (_rms(oa_ref[0]) * ga_ref[...]).astype(BF16)
    nb = (_rms(ob_ref[0]) * gb_ref[...]).astype(BF16)
    attn = _dot(na, wout_ref[0:D_SB, :]) + _dot(nb, wout_ref[D_SB:, :])
    x1 = x_ref[0] + ga1 * attn
    h = ((_rms(x1) * gm_ref[...]) * (1.0 + sc2) + sh2).astype(BF16)
    acc = jnp.zeros_like(x1)
    for cblk in range(D_FF // FF_TILE):
        u = _dot(h, wup_ref[:, cblk * FF_TILE:(cblk + 1) * FF_TILE])
        r = jnp.maximum(u, 0.0)
        acc = acc + _dot((r * r).astype(BF16), wdn_ref[cblk * FF_TILE:(cblk + 1) * FF_TILE, :])
    x2 = x1 + ga2 * acc
    if final:
        x2 = _rms(x2) * gf_ref[...]
    o_ref[0] = x2


def _post_call(o_a, o_b, x, modl, g_a, g_b, w_out, g_mlp, w_up, w_down, g_final, final):
    bsz, s, d = x.shape
    tm = min(TOKEN_TILE, s)
    const2 = lambda b, i: (0, 0)
    return pl.pallas_call(
        functools.partial(_post_body, final=final),
        grid=(bsz, s // tm),
        in_specs=[
            pl.BlockSpec((1, tm, D_SB), lambda b, i: (b, i, 0)),
            pl.BlockSpec((1, tm, D_DSA_OUT), lambda b, i: (b, i, 0)),
            pl.BlockSpec((1, tm, d), lambda b, i: (b, i, 0)),
            pl.BlockSpec((1, 6, d), lambda b, i: (b, 0, 0)),
            pl.BlockSpec((1, D_SB), const2),
            pl.BlockSpec((1, D_DSA_OUT), const2),
            pl.BlockSpec((d, d), const2),
            pl.BlockSpec((1, d), const2),
            pl.BlockSpec((d, D_FF), const2),
            pl.BlockSpec((D_FF, d), const2),
            pl.BlockSpec((1, d), const2),
        ],
        out_specs=pl.BlockSpec((1, tm, d), lambda b, i: (b, i, 0)),
        out_shape=jax.ShapeDtypeStruct((bsz, s, d), F32),
        compiler_params=pltpu.CompilerParams(
            dimension_semantics=("parallel", "parallel"), vmem_limit_bytes=VMEM_LIMIT),
        name="outproj_mlp",
    )(o_a, o_b, x, modl, g_a.reshape(1, -1), g_b.reshape(1, -1), w_out, g_mlp.reshape(1, d),
      w_up, w_down, g_final.reshape(1, d))


def _t5_bucket_table(rel):
    nb = N_BUCKETS // 2
    max_exact = nb // 2
    n = np.abs(rel)
    nf = np.maximum(n, max_exact).astype(np.float64)
    large = max_exact + (np.log(nf / max_exact) / math.log(MAX_DISTANCE / max_exact)
                         * (nb - max_exact)).astype(np.int64)
    large = np.minimum(large, nb - 1)
    return np.where(rel > 0, nb, 0) + np.where(n < max_exact, n, large)


def _toeplitz_key_query(vec, t):
    width = 2 * t
    ext = jnp.pad(vec, ((0, 0), (0, 1)))
    flat = jnp.tile(ext, (1, t))[:, :t * (width - 1)]
    by_query = flat.reshape(vec.shape[0], t, width - 1)[:, :, t - 1:]
    return jnp.swapaxes(by_query, 1, 2)


def _bias_tiles(rel_bias, t, s):
    far_bucket = _t5_bucket_table(np.arange(-(s - 1), -t + 1))
    assert t >= MAX_DISTANCE and np.all(far_bucket == far_bucket[0])
    offs = np.arange(-(t - 1), t)
    tab = rel_bias.astype(F32)
    far = tab[int(far_bucket[0])]
    vec_diag = (tab[_t5_bucket_table(offs)] - far).T * LOG2E
    vec_prev = (tab[_t5_bucket_table(offs - t)] - far).T * LOG2E
    prev, diag = _toeplitz_key_query(vec_prev, t), _toeplitz_key_query(vec_diag, t)
    return jnp.concatenate([prev, diag], axis=1), jnp.stack([jnp.zeros_like(diag), diag])


def _cat_weight(w_in_l):
    c_sb = 3 * D_SB
    c_dq = c_sb + D_DSA_Q
    c_kv = c_dq + DSA_KV_RANK
    c_iq = c_kv + D_IDX_Q
    c_ik = c_iq + IDX_DIM
    kv = w_in_l[:, c_dq:c_kv]
    iq = w_in_l[:, c_kv:c_iq]
    ik = w_in_l[:, c_iq:c_ik]
    iw = w_in_l[:, c_ik:]
    pad = jnp.zeros((w_in_l.shape[0], LANES - IDX_HEADS), w_in_l.dtype)
    return jnp.concatenate([w_in_l[:, :c_dq], iq, kv, ik, ik, iw, pad], axis=1).astype(BF16)


def kernel(x, c, w_mod, b_mod, g_attn, w_in, kv_norm_g, w_uv, g_out_a, g_out_b, w_out, g_mlp,
           w_up, w_down, rel_bias, g_final):
    bsz, s, d = x.shape
    assert d == D_MODEL and s % TOKEN_TILE == 0 and TOKEN_TILE % ATT_TILE == 0
    mod = _mod_call(c, w_mod, b_mod)
    bias_near, bias_lone = _bias_tiles(rel_bias, ATT_TILE, s)
    tri = jnp.asarray(np.tril(np.ones((ATT_TILE, ATT_TILE), np.float32), -1), BF16)
    for l in range(DEPTH):
        modl = mod[l].reshape(bsz, 6, d)
        qa, ka, va, dq, kv, kvt, iq, ik2, iwt = _inproj_call(
            x, modl, g_attn[l], _cat_weight(w_in[l]), kv_norm_g[l])
        o_a = _sb_call(qa, ka, va, tri)
        wuvt = jnp.transpose(w_uv[l], (0, 2, 1)).astype(BF16)
        o_b = _dsa_call(dq, kv, kvt, iq, ik2, iwt, bias_near, bias_lone, wuvt, tri)
        x = _post_call(o_a, o_b, x, modl, g_out_a[l], g_out_b[l], w_out[l].astype(BF16), g_mlp[l],
                       w_up[l].astype(BF16), w_down[l].astype(BF16), g_final, l == DEPTH - 1)
    return x
```

```python
import functools
import math

import numpy as np
import jax
import jax.numpy as jnp
from jax import lax
from jax.experimental import pallas as pl
from jax.experimental.pallas import tpu as pltpu

F32 = jnp.float32
BF16 = jnp.bfloat16
I32 = jnp.int32
I16 = jnp.int16

D_MODEL = 1024
DEPTH = 2
CHUNK = 64
SB_HEADS = 8
SB_HEAD_DIM = 64
DSA_HEADS = 8
DSA_HEAD_DIM = 64
DSA_KV_RANK = 128
IDX_HEADS = 8
IDX_DIM = 64
DSA_MAX_TOPK = 256
D_FF = 4 * D_MODEL
N_BUCKETS = 32
MAX_DISTANCE = 128
EPS = 1e-6

D_SB = SB_HEADS * SB_HEAD_DIM
D_DSA_Q = DSA_HEADS * DSA_KV_RANK
D_IDX_Q = IDX_HEADS * IDX_DIM
D_DSA_OUT = DSA_HEADS * DSA_HEAD_DIM

LANES = 128
SUBLANES = 8
ATT_TILE = 256
SB_Q_TILE = 2 * ATT_TILE
COUNT_CHAINS = 4
TOKEN_TILE = 512
FF_TILE = 1024
MOD_TILE = 1024
VMEM_LIMIT = 56 * 1024 * 1024

NEG_BIG = -(2.0 ** 100)
ONES_ROWS = 16
INT_MIN = -2 ** 31
HALF_BIAS = 2 ** 15
RANK_NEVER = 1e9
LOG2E = 1.4426950408889634

C_QA, C_KA, C_VA = 0, D_SB, 2 * D_SB
C_DQ = 3 * D_SB
C_IQ = C_DQ + D_DSA_Q
C_KV = C_IQ + D_IDX_Q
C_IK = C_KV + DSA_KV_RANK
C_IW = C_IK + LANES
D_CAT = C_IW + LANES


def _nt(a, b):
    return lax.dot_general(a, b, (((1,), (1,)), ((), ())), preferred_element_type=F32)


def _dot(a, b):
    return jnp.dot(a, b, preferred_element_type=F32)


def _rms(v, axis=-1):
    return v * lax.rsqrt(jnp.mean(v * v, axis=axis, keepdims=True) + EPS)


def _float_to_key(x):
    bits = lax.bitcast_convert_type(x, I32)
    return jnp.where(bits < 0, jnp.int32(INT_MIN) - bits, bits)


def _key_to_float(key):
    return lax.bitcast_convert_type(jnp.where(key < 0, jnp.int32(INT_MIN) - key, key), F32)


def _mod_body(c_ref, w_ref, b_ref, o_ref):
    c = c_ref[...]
    ca = c / (1.0 + jnp.exp(-c))
    c_hi = ca.astype(BF16)
    c_lo = (ca - c_hi.astype(F32)).astype(BF16)
    w = w_ref[0]
    w_hi = w.astype(BF16)
    w_lo = (w - w_hi.astype(F32)).astype(BF16)
    o_ref[0] = _dot(c_hi, w_hi) + _dot(c_lo, w_hi) + _dot(c_hi, w_lo) + b_ref[0]


def _mod_call(c, w_mod, b_mod):
    depth, d, n = w_mod.shape
    bsz = c.shape[0]
    return pl.pallas_call(
        _mod_body,
        grid=(depth, n // MOD_TILE),
        in_specs=[
            pl.BlockSpec((bsz, d), lambda l, j: (0, 0)),
            pl.BlockSpec((1, d, MOD_TILE), lambda l, j: (l, 0, j)),
            pl.BlockSpec((1, 1, MOD_TILE), lambda l, j: (l, 0, j)),
        ],
        out_specs=pl.BlockSpec((1, bsz, MOD_TILE), lambda l, j: (l, 0, j)),
        out_shape=jax.ShapeDtypeStruct((depth, bsz, n), F32),
        compiler_params=pltpu.CompilerParams(vmem_limit_bytes=VMEM_LIMIT),
        name="adaln_mod",
    )(c, w_mod, b_mod.reshape(depth, 1, n))


def _inproj_body(x_ref, mod_ref, g_ref, w_ref, kvg_ref,
                 qa_ref, ka_ref, va_ref, dq_ref, kv_ref, kvt_ref, iq_ref, ik_ref, iwt_ref):
    x = x_ref[0]
    sh1 = mod_ref[0, 0:1, :]
    sc1 = mod_ref[0, 1:2, :]
    h = (_rms(x) * g_ref[...]) * (1.0 + sc1) + sh1
    h = h.astype(BF16)

    def proj(c0, width):
        return _dot(h, w_ref[:, c0:c0 + width])

    qa_ref[0] = (proj(C_QA, D_SB) * (SB_HEAD_DIM ** -0.5 * LOG2E)).astype(BF16)
    ka_ref[0] = proj(C_KA, D_SB).astype(BF16)
    va_ref[0] = proj(C_VA, D_SB).astype(BF16)
    dq = proj(C_DQ, D_DSA_Q) * (DSA_KV_RANK ** -0.5 * LOG2E)
    for hd in range(DSA_HEADS):
        dq_ref[0, hd] = dq[:, hd * DSA_KV_RANK:(hd + 1) * DSA_KV_RANK].astype(BF16)
    iq = proj(C_IQ, D_IDX_Q).astype(BF16)
    for hp in range(IDX_HEADS // 2):
        iq_ref[0, hp] = iq[:, hp * LANES:(hp + 1) * LANES]
    rest = proj(C_KV, D_CAT - C_KV)
    kv = _rms(rest[:, 0:DSA_KV_RANK]) * kvg_ref[...]
    kv_ref[0] = kv.astype(BF16)
    kv_t = kv.T.astype(BF16)
    ones = jnp.ones((ONES_ROWS, ATT_TILE), BF16)
    for cblk in range(kvt_ref.shape[1]):
        kvt_ref[0, cblk] = jnp.concatenate([kv_t[:, cblk * ATT_TILE:(cblk + 1) * ATT_TILE], ones], axis=0)
    ik_ref[0] = rest[:, C_IK - C_KV:C_IK - C_KV + LANES].astype(BF16)
    iwt_ref[0] = rest[:, C_IW - C_KV:].T[0:IDX_HEADS, :]


def _inproj_call(x, modl, g, wcat, kvg):
    bsz, s, d = x.shape
    tm = min(TOKEN_TILE, s)
    nkb = tm // ATT_TILE
    out_shapes = (
        jax.ShapeDtypeStruct((bsz, s, D_SB), BF16),
        jax.ShapeDtypeStruct((bsz, s, D_SB), BF16),
        jax.ShapeDtypeStruct((bsz, s, D_SB), BF16),
        jax.ShapeDtypeStruct((bsz, DSA_HEADS, s, DSA_KV_RANK), BF16),
        jax.ShapeDtypeStruct((bsz, s, DSA_KV_RANK), BF16),
        jax.ShapeDtypeStruct((bsz, s // ATT_TILE, DSA_KV_RANK + ONES_ROWS, ATT_TILE), BF16),
        jax.ShapeDtypeStruct((bsz, IDX_HEADS // 2, s, LANES), BF16),
        jax.ShapeDtypeStruct((bsz, s, LANES), BF16),
        jax.ShapeDtypeStruct((bsz, IDX_HEADS, s), F32),
    )
    out_specs = (
        pl.BlockSpec((1, tm, D_SB), lambda b, i: (b, i, 0)),
        pl.BlockSpec((1, tm, D_SB), lambda b, i: (b, i, 0)),
        pl.BlockSpec((1, tm, D_SB), lambda b, i: (b, i, 0)),
        pl.BlockSpec((1, DSA_HEADS, tm, DSA_KV_RANK), lambda b, i: (b, 0, i, 0)),
        pl.BlockSpec((1, tm, DSA_KV_RANK), lambda b, i: (b, i, 0)),
        pl.BlockSpec((1, nkb, DSA_KV_RANK + ONES_ROWS, ATT_TILE), lambda b, i: (b, i, 0, 0)),
        pl.BlockSpec((1, IDX_HEADS // 2, tm, LANES), lambda b, i: (b, 0, i, 0)),
        pl.BlockSpec((1, tm, LANES), lambda b, i: (b, i, 0)),
        pl.BlockSpec((1, IDX_HEADS, tm), lambda b, i: (b, 0, i)),
    )
    return pl.pallas_call(
        _inproj_body,
        grid=(bsz, s // tm),
        in_specs=[
            pl.BlockSpec((1, tm, d), lambda b, i: (b, i, 0)),
            pl.BlockSpec((1, 6, d), lambda b, i: (b, 0, 0)),
            pl.BlockSpec((1, d), lambda b, i: (0, 0)),
            pl.BlockSpec((d, D_CAT), lambda b, i: (0, 0)),
            pl.BlockSpec((1, DSA_KV_RANK), lambda b, i: (0, 0)),
        ],
        out_specs=out_specs,
        out_shape=out_shapes,
        compiler_params=pltpu.CompilerParams(
            dimension_semantics=("parallel", "parallel"), vmem_limit_bytes=VMEM_LIMIT),
        name="inproj",
    )(x, modl, g.reshape(1, d), wcat, kvg.reshape(1, DSA_KV_RANK))


def _sb_body(q_ref, k_ref, v_ref, tri_ref, o_ref, acc_ref, *, tq, tk, nq):
    lane = lax.broadcasted_iota(I32, (1, LANES), 1)
    in_half = (lane < SB_HEAD_DIM, lane >= SB_HEAD_DIM)
    tri = tri_ref[...]
    acc_ref[...] = jnp.zeros(acc_ref.shape, F32)

    def tile(k, j, carries, r0, diag_off):
        kb = k_ref[0, j * tk:(j + 1) * tk, :]
        vb = v_ref[0, j * tk:(j + 1) * tk, :]
        q_rows = slice(k * tq + r0, (k + 1) * tq)
        q2 = q_ref[0, q_rows, :]
        if diag_off is not None:
            row = lax.broadcasted_iota(I32, (tq - r0, tk), 0) + r0
            col = lax.broadcasted_iota(I32, (tq - r0, tk), 1) + diag_off
            causal = col < row
        out = []
        for half in range(2):
            kh = jnp.where(in_half[half], kb, jnp.zeros_like(kb))
            z = _nt(q2, kh)
            z_neg = jnp.minimum(z, 0.0)
            sp = jnp.log(1.0 + jnp.exp2(z_neg + z_neg - z)) * LOG2E
            log_beta = z_neg - sp
            log_1m = log_beta - z
            if diag_off is not None:
                log_1m = jnp.where(causal, log_1m, 0.0)
            after = _dot(log_1m.astype(BF16), tri)
            carry = carries[half][r0:tq]
            w = jnp.exp2(log_beta + after + carry)
            if diag_off is not None:
                w = jnp.where(causal, w, 0.0)
            acc_ref[half, q_rows, :] += _dot(w.astype(BF16), vb)
            carry = carry + jnp.sum(log_1m, axis=-1, keepdims=True)
            if r0 > 0:
                carry = jnp.concatenate([carries[half][0:r0], carry], axis=0)
            out.append(carry)
        return tuple(out)

    n_diag = tq // tk

    for k in range(nq):
        zero = jnp.zeros((tq, 1), F32)
        carries = (zero, zero)
        for d in range(n_diag - 1, -1, -1):
            carries = tile(k, k * n_diag + d, carries, d * tk, d * tk)
        for j in range(k * n_diag - 1, -1, -1):
            carries = tile(k, j, carries, 0, None)
    o_ref[0] = jnp.where(in_half[0], acc_ref[0], acc_ref[1])


def _sb_call(qa, ka, va, tri):
    bsz, s, _ = qa.shape
    tq, tk = min(SB_Q_TILE, s), ATT_TILE
    return pl.pallas_call(
        functools.partial(_sb_body, tq=tq, tk=tk, nq=s // tq),
        grid=(bsz, D_SB // LANES),
        in_specs=[
            pl.BlockSpec((1, s, LANES), lambda b, hp: (b, 0, hp)),
            pl.BlockSpec((1, s, LANES), lambda b, hp: (b, 0, hp)),
            pl.BlockSpec((1, s, LANES), lambda b, hp: (b, 0, hp)),
            pl.BlockSpec((tk, tk), lambda b, hp: (0, 0)),
        ],
        out_specs=pl.BlockSpec((1, s, LANES), lambda b, hp: (b, 0, hp)),
        out_shape=jax.ShapeDtypeStruct((bsz, s, D_SB), F32),
        scratch_shapes=[pltpu.VMEM((2, s, LANES), F32)],
        compiler_params=pltpu.CompilerParams(
            dimension_semantics=("parallel", "parallel"), vmem_limit_bytes=VMEM_LIMIT),
        name="stickbreak_attn",
    )(qa, ka, va, tri)


def _dsa_body(dq_ref, kv_ref, kvt_ref, iq_ref, ik_ref, iwt_ref, bias_near_ref, bias_diag_ref, wuvt_ref,
              tri_ref, o_ref, score_ref, hi_ref, lo_ref, madd_ref, acc_ref, obt_ref, *, t, topk, nkb):
    i = pl.program_id(1)
    nblk = i + 1
    lane = lax.broadcasted_iota(I32, (1, LANES), 1)
    lo_half = lane < IDX_DIM
    srow = lax.broadcasted_iota(I32, (t, t), 0)
    qcol = lax.broadcasted_iota(I32, (t, t), 1)

    iq_pairs = iq_ref[0].reshape((IDX_HEADS // 2) * t, LANES)

    def score_keys(j):
        start = j * t
        ikb = ik_ref[0, pl.ds(start, t), :]
        zero = jnp.zeros_like(ikb)
        sc_half = (_nt(jnp.where(lo_half, ikb, zero), iq_pairs),
                   _nt(jnp.where(lo_half, zero, ikb), iq_pairs))
        acc = jnp.zeros((t, t), F32)
        for hd in range(IDX_HEADS):
            sc = sc_half[hd % 2][:, (hd // 2) * t:(hd // 2 + 1) * t]
            acc = acc + jnp.maximum(sc, 0.0) * iwt_ref[0, hd:hd + 1, :]
        return acc, start

    def store_scores(scores, start):
        score_ref[pl.ds(start, t), :] = scores
        keys = _float_to_key(scores)
        hi_ref[pl.ds(start, t), :] = (keys >> 16).astype(I16)
        lo_ref[pl.ds(start, t), :] = ((keys & 0xFFFF) - HALF_BIAS).astype(I16)

    def score_tiles(diag):
        def run():
            for j in range(diag):
                store_scores(*score_keys(j))
            scores, start = score_keys(diag)
            admissible = (srow >> 6) <= (qcol >> 6)
            store_scores(jnp.where(admissible, scores, -jnp.inf), start)
        return run

    lax.switch(i, [score_tiles(n) for n in range(nkb)])

    group16 = 2 * SUBLANES * COUNT_CHAINS

    def col_count16(ref, cand):
        def body(j, c):
            blk = ref[pl.ds(pl.multiple_of(j * t, t), t), :]
            ind = jnp.where(blk >= cand, jnp.int16(1), jnp.int16(0))
            for r in range(t // group16):
                c = c + ind[r * group16:(r + 1) * group16, :]
            return c
        c = lax.fori_loop(0, nblk, body, jnp.zeros((group16, t), I16))
        return jnp.sum(c.astype(I32), axis=0, keepdims=True)

    def half_search(ref, want):
        def bit_step(n, thr):
            cand = thr + lax.shift_left(jnp.int32(1), 15 - n)
            cnt = col_count16(ref, cand.astype(I16))
            return jnp.where(cnt >= want, cand, thr)
        return lax.fori_loop(0, 16, bit_step, jnp.full((1, t), -HALF_BIAS, I32))

    thr_hi = half_search(hi_ref, topk)
    above = col_count16(hi_ref, (thr_hi + 1).astype(I16))
    above = jnp.where(thr_hi == HALF_BIAS - 1, 0, above)
    thr_hi16 = thr_hi.astype(I16)

    def keep_bucket(j, carry):
        rows = pl.ds(pl.multiple_of(j * t, t), t)
        lo_ref[rows, :] = jnp.where(hi_ref[rows, :] == thr_hi16, lo_ref[rows, :], jnp.int16(-HALF_BIAS))
        return carry

    lax.fori_loop(0, nblk, keep_bucket, 0)
    thr_lo = half_search(lo_ref, topk - above)
    thr_packed = _key_to_float((thr_hi << 16) | (thr_lo + HALF_BIAS))

    group = SUBLANES * COUNT_CHAINS

    def col_count(pred_fn):
        def body(j, c):
            ind = pred_fn(score_ref[pl.ds(pl.multiple_of(j * t, t), t), :]).astype(I32)
            return c + jnp.sum(ind.reshape(t // group, group, t), axis=0)
        c = lax.fori_loop(0, nblk, body, jnp.zeros((group, t), I32))
        return jnp.sum(c, axis=0, keepdims=True)

    def float_search():
        def bit_step(n, key):
            cand = key + lax.shift_left(jnp.int32(1), 31 - n)
            cand_f = _key_to_float(cand)
            return jnp.where(col_count(lambda sb: sb >= cand_f) >= topk, cand, key)
        return _key_to_float(lax.fori_loop(0, 32, bit_step, jnp.full((1, t), INT_MIN, I32)))

    cnt_ge = col_count(lambda sb: sb >= thr_packed)
    cnt_gt = col_count(lambda sb: sb > thr_packed)
    confirmed = jnp.logical_and(cnt_ge >= topk, cnt_gt < topk)
    all_confirmed = jnp.min(confirmed.astype(I32)) > 0
    thr, cnt_gt = lax.cond(
        all_confirmed, lambda: (thr_packed, cnt_gt),
        lambda: (lambda th: (th, col_count(lambda sb: sb > th)))(float_search()))

    real_thr = thr > -jnp.inf
    need = jnp.where(real_thr, topk - cnt_gt, 0).astype(F32)
    tri = tri_ref[...]

    def store_masks(n_tiles):
        def run():
            seen = jnp.zeros((1, t), F32)
            for j in range(n_tiles):
                sb = score_ref[j * t:(j + 1) * t, :]
                is_eq = sb == thr
                eq = jnp.where(is_eq, 1.0, 0.0)
                rank = _dot(tri, eq.astype(BF16)) + seen
                order = jnp.where(is_eq, rank, jnp.where(sb > thr, -1.0, RANK_NEVER))
                madd_ref[j * t:(j + 1) * t, :] = jnp.where(order < need, 0.0, NEG_BIG).astype(BF16)
                seen = seen + jnp.sum(eq, axis=0, keepdims=True)
        return run

    lax.switch(i, [store_masks(n + 1) for n in range(nkb)])

    width = DSA_HEADS * t
    q_all = dq_ref[0].reshape(width, DSA_KV_RANK)
    acc_ref[...] = jnp.zeros(acc_ref.shape, F32)

    def attend(j0, n, m_old, bias_of_head):
        rows = pl.ds(pl.multiple_of(j0 * t, t), n * t)
        s_all = _nt(kv_ref[0, rows, :], q_all).astype(BF16)
        madd = madd_ref[rows, :]
        s_heads = []
        for hd in range(DSA_HEADS):
            s = s_all[:, hd * t:(hd + 1) * t] + madd
            bias = bias_of_head(hd)
            if bias is not None:
                s = s + bias
            s_heads.append(s)
        s_all = jnp.concatenate(s_heads, axis=1)
        m_new = jnp.maximum(m_old, jnp.max(s_all, axis=0, keepdims=True).astype(F32))
        alpha = jnp.exp2(m_old - m_new)
        p = jnp.exp2(s_all - m_new.astype(BF16))
        kvt = jnp.concatenate([kvt_ref[0, j0 + d] for d in range(n)], axis=1)
        acc_ref[...] = acc_ref[...] * alpha + _dot(kvt, p)
        return m_new

    m_run = jnp.full((1, width), NEG_BIG, F32)
    m_run = lax.cond(i == 0, lambda m: attend(0, 1, m, lambda hd: bias_diag_ref[hd]), lambda m: m, m_run)
    m_run = lax.cond(jnp.logical_and(i % 2 == 0, i > 0),
                     lambda m: attend(0, 1, m, lambda hd: None), lambda m: m, m_run)
    first = 1 - i % 2
    n_pairs = (i + 1) // 2
    m_run = lax.fori_loop(0, jnp.maximum(n_pairs - 1, 0),
                          lambda k, m: attend(first + 2 * k, 2, m, lambda hd: None), m_run)
    lax.cond(i >= 1, lambda m: attend(i - 1, 2, m, lambda hd: bias_near_ref[hd]), lambda m: m, m_run)

    l_fin = acc_ref[DSA_KV_RANK:DSA_KV_RANK + 1, :]
    o_t = (acc_ref[0:DSA_KV_RANK, :] * (1.0 / l_fin)).astype(BF16)
    for hd in range(DSA_HEADS):
        obt_ref[hd * DSA_HEAD_DIM:(hd + 1) * DSA_HEAD_DIM, :] = _dot(
            wuvt_ref[hd], o_t[:, hd * t:(hd + 1) * t])
    o_ref[0] = obt_ref[...].T


def _dsa_call(dq, kv, kvt, iq, ik2, iwt, bias_near, bias_diag, wuvt, tri):
    bsz, s, _ = kv.shape
    t = ATT_TILE
    nkb = s // t
    topk = min(DSA_MAX_TOPK, s // 4)
    body = functools.partial(_dsa_body, t=t, topk=topk, nkb=nkb)
    return pl.pallas_call(
        body,
        grid=(bsz, nkb),
        in_specs=[
            pl.BlockSpec((1, DSA_HEADS, t, DSA_KV_RANK), lambda b, i: (b, 0, i, 0)),
            pl.BlockSpec((1, s, DSA_KV_RANK), lambda b, i: (b, 0, 0)),
            pl.BlockSpec((1, nkb, DSA_KV_RANK + ONES_ROWS, t), lambda b, i: (b, 0, 0, 0)),
            pl.BlockSpec((1, IDX_HEADS // 2, t, LANES), lambda b, i: (b, 0, i, 0)),
            pl.BlockSpec((1, s, LANES), lambda b, i: (b, 0, 0)),
            pl.BlockSpec((1, IDX_HEADS, t), lambda b, i: (b, 0, i)),
            pl.BlockSpec((DSA_HEADS, 2 * t, t), lambda b, i: (0, 0, 0)),
            pl.BlockSpec((DSA_HEADS, t, t), lambda b, i: (0, 0, 0)),
            pl.BlockSpec((DSA_HEADS, DSA_HEAD_DIM, DSA_KV_RANK), lambda b, i: (0, 0, 0)),
            pl.BlockSpec((t, t), lambda b, i: (0, 0)),
        ],
        out_specs=pl.BlockSpec((1, t, D_DSA_OUT), lambda b, i: (b, i, 0)),
        out_shape=jax.ShapeDtypeStruct((bsz, s, D_DSA_OUT), F32),
        scratch_shapes=[
            pltpu.VMEM((s, t), F32),
            pltpu.VMEM((s, t), I16),
            pltpu.VMEM((s, t), I16),
            pltpu.VMEM((s, t), BF16),
            pltpu.VMEM((DSA_KV_RANK + ONES_ROWS, DSA_HEADS * t), F32),
            pltpu.VMEM((D_DSA_OUT, t), F32),
        ],
        compiler_params=pltpu.CompilerParams(
            dimension_semantics=("parallel", "parallel"), vmem_limit_bytes=VMEM_LIMIT),
        name="sparse_attn",
    )(dq, kv, kvt, iq, ik2, iwt, bias_near, bias_diag, wuvt, tri)


def _post_body(oa_ref, ob_ref, x_ref, mod_ref, ga_ref, gb_ref, wout_ref, gm_ref, wup_ref, wdn_ref,
               gf_ref, o_ref, *, final):
    ga1 = mod_ref[0, 2:3, :]
    sh2 = mod_ref[0, 3:4, :]
    sc2 = mod_ref[0, 4:5, :]
    ga2 = mod_ref[0, 5:6, :]
    na = (_rms(oa_ref[0]) * ga_ref[...]).astype(BF16)
    nb = (_rms(ob_ref[0]) * gb_ref[...]).astype(BF16)
    attn = _dot(na, wout_ref[0:D_SB, :]) + _dot(nb, wout_ref[D_SB:, :])
    x1 = x_ref[0] + ga1 * attn
    h = ((_rms(x1) * gm_ref[...]) * (1.0 + sc2) + sh2).astype(BF16)
    acc = jnp.zeros_like(x1)
    for cblk in range(D_FF // FF_TILE):
        u = _dot(h, wup_ref[:, cblk * FF_TILE:(cblk + 1) * FF_TILE])
        r = jnp.maximum(u, 0.0)
        acc = acc + _dot((r * r).astype(BF16), wdn_ref[cblk * FF_TILE:(cblk + 1) * FF_TILE, :])
    x2 = x1 + ga2 * acc
    if final:
        x2 = _rms(x2) * gf_ref[...]
    o_ref[0] = x2


def _post_call(o_a, o_b, x, modl, g_a, g_b, w_out, g_mlp, w_up, w_down, g_final, final):
    bsz, s, d = x.shape
    tm = min(TOKEN_TILE, s)
    const2 = lambda b, i: (0, 0)
    return pl.pallas_call(
        functools.partial(_post_body, final=final),
        grid=(bsz, s // tm),
        in_specs=[
            pl.BlockSpec((1, tm, D_SB), lambda b, i: (b, i, 0)),
            pl.BlockSpec((1, tm, D_DSA_OUT), lambda b, i: (b, i, 0)),
            pl.BlockSpec((1, tm, d), lambda b, i: (b, i, 0)),
            pl.BlockSpec((1, 6, d), lambda b, i: (b, 0, 0)),
            pl.BlockSpec((1, D_SB), const2),
            pl.BlockSpec((1, D_DSA_OUT), const2),
            pl.BlockSpec((d, d), const2),
            pl.BlockSpec((1, d), const2),
            pl.BlockSpec((d, D_FF), const2),
            pl.BlockSpec((D_FF, d), const2),
            pl.BlockSpec((1, d), const2),
        ],
        out_specs=pl.BlockSpec((1, tm, d), lambda b, i: (b, i, 0)),
        out_shape=jax.ShapeDtypeStruct((bsz, s, d), F32),
        compiler_params=pltpu.CompilerParams(
            dimension_semantics=("parallel", "parallel"), vmem_limit_bytes=VMEM_LIMIT),
        name="outproj_mlp",
    )(o_a, o_b, x, modl, g_a.reshape(1, -1), g_b.reshape(1, -1), w_out, g_mlp.reshape(1, d),
      w_up, w_down, g_final.reshape(1, d))


def _t5_bucket_table(rel):
    nb = N_BUCKETS // 2
    max_exact = nb // 2
    n = np.abs(rel)
    nf = np.maximum(n, max_exact).astype(np.float64)
    large = max_exact + (np.log(nf / max_exact) / math.log(MAX_DISTANCE / max_exact)
                         * (nb - max_exact)).astype(np.int64)
    large = np.minimum(large, nb - 1)
    return np.where(rel > 0, nb, 0) + np.where(n < max_exact, n, large)


def _toeplitz_key_query(vec, t):
    width = 2 * t
    ext = jnp.pad(vec, ((0, 0), (0, 1)))
    flat = jnp.tile(ext, (1, t))[:, :t * (width - 1)]
    by_query = flat.reshape(vec.shape[0], t, width - 1)[:, :, t - 1:]
    return jnp.swapaxes(by_query, 1, 2)


def _bias_tiles(rel_bias, t, s):
    far_bucket = _t5_bucket_table(np.arange(-(s - 1), -t + 1))
    assert t >= MAX_DISTANCE and np.all(far_bucket == far_bucket[0])
    offs = np.arange(-(t - 1), t)
    tab = rel_bias.astype(F32)
    far = tab[int(far_bucket[0])]
    vec_diag = (tab[_t5_bucket_table(offs)] - far).T * LOG2E
    vec_prev = (tab[_t5_bucket_table(offs - t)] - far).T * LOG2E
    prev = _toeplitz_key_query(vec_prev, t).astype(BF16)
    diag = _toeplitz_key_query(vec_diag, t).astype(BF16)
    return jnp.concatenate([prev, diag], axis=1), diag


def _cat_weight(w_in_l):
    c_sb = 3 * D_SB
    c_dq = c_sb + D_DSA_Q
    c_kv = c_dq + DSA_KV_RANK
    c_iq = c_kv + D_IDX_Q
    c_ik = c_iq + IDX_DIM
    kv = w_in_l[:, c_dq:c_kv]
    iq = w_in_l[:, c_kv:c_iq]
    ik = w_in_l[:, c_iq:c_ik]
    iw = w_in_l[:, c_ik:]
    pad = jnp.zeros((w_in_l.shape[0], LANES - IDX_HEADS), w_in_l.dtype)
    return jnp.concatenate([w_in_l[:, :c_dq], iq, kv, ik, ik, iw, pad], axis=1).astype(BF16)


def kernel(x, c, w_mod, b_mod, g_attn, w_in, kv_norm_g, w_uv, g_out_a, g_out_b, w_out, g_mlp,
           w_up, w_down, rel_bias, g_final):
    bsz, s, d = x.shape
    assert d == D_MODEL and s % TOKEN_TILE == 0 and TOKEN_TILE % ATT_TILE == 0
    mod = _mod_call(c, w_mod, b_mod)
    bias_near, bias_diag = _bias_tiles(rel_bias, ATT_TILE, s)
    tri = jnp.asarray(np.tril(np.ones((ATT_TILE, ATT_TILE), np.float32), -1), BF16)
    for l in range(DEPTH):
        modl = mod[l].reshape(bsz, 6, d)
        qa, ka, va, dq, kv, kvt, iq, ik2, iwt = _inproj_call(
            x, modl, g_attn[l], _cat_weight(w_in[l]), kv_norm_g[l])
        o_a = _sb_call(qa, ka, va, tri)
        wuvt = jnp.transpose(w_uv[l], (0, 2, 1)).astype(BF16)
        o_b = _dsa_call(dq, kv, kvt, iq, ik2, iwt, bias_near, bias_diag, wuvt, tri)
        x = _post_call(o_a, o_b, x, modl, g_out_a[l], g_out_b[l], w_out[l].astype(BF16), g_mlp[l],
                       w_up[l].astype(BF16), w_down[l].astype(BF16), g_final, l == DEPTH - 1)
    return x
```
